```python
import math
import jax, jax.numpy as jnp
from jax import lax
import numpy as np

D_MODEL = 1024
BATCH = 16
SEQ = 4096
DEPTH = 1

HGRN_HEADS = 4
HGRN_DK = 128
HGRN_DV = 128
HGRN_WIDTH = HGRN_HEADS * HGRN_DV
DIFF_HEADS = 4
DIFF_DH = 64
DIFF_DV = 2 * DIFF_DH
DIFF_WIDTH = DIFF_HEADS * DIFF_DV
MIX_WIDTH = HGRN_WIDTH + DIFF_WIDTH
IN_WIDTH = 4 * HGRN_WIDTH + 3 * DIFF_WIDTH
D_FF = -(-8 * D_MODEL // (3 * 256)) * 256
N_BUCKETS = 32
MAX_DISTANCE = 128
CHUNK = 64
Q_BLOCK = 128
LN_EPS = 1e-5
RMS_EPS = 1e-6

kernel_name = "hybrid_hgrn2_diffattn_deepnorm_adaln"


def _layer_norm(x, eps=LN_EPS):
    xf = x.astype(jnp.float32)
    mu = jnp.mean(xf, axis=-1, keepdims=True)
    var = jnp.mean(jnp.square(xf - mu), axis=-1, keepdims=True)
    return ((xf - mu) * lax.rsqrt(var + eps)).astype(x.dtype)


def _rms_norm(x, w, eps=RMS_EPS):
    xf = x.astype(jnp.float32)
    y = xf * lax.rsqrt(jnp.mean(jnp.square(xf), axis=-1, keepdims=True) + eps)
    return (y * w.astype(jnp.float32)).astype(x.dtype)


def _t5_bucket(dist):
    max_exact = N_BUCKETS // 2
    d = jnp.maximum(dist, 1).astype(jnp.float32)
    large = max_exact + (jnp.log(d / max_exact) / math.log(MAX_DISTANCE / max_exact)
                         * (N_BUCKETS - max_exact)).astype(jnp.int32)
    large = jnp.minimum(large, N_BUCKETS - 1)
    return jnp.where(dist < max_exact, dist, large)


def _hgrn2_chunkwise(q, f_pre, i, lb):
    B, S, H, DK = q.shape
    DV = i.shape[-1]
    nc = S // CHUNK

    def chunked(t):
        return t.astype(jnp.float32).reshape(B, nc, CHUNK, H, t.shape[-1]).transpose(0, 3, 1, 2, 4)

    fp = chunked(f_pre)
    qc = chunked(q) * (DK ** -0.5)
    vc = chunked(i)
    lbb = lb.astype(jnp.float32)[None, :, None, None, :]
    log_f = jnp.log(lbb + (1.0 - lbb) * jax.nn.sigmoid(fp))
    kc = (1.0 - lbb) * jax.nn.sigmoid(-fp)
    b = jnp.cumsum(log_f, axis=3)
    b_last = b[:, :, :, -1:, :]
    b_mid = b[:, :, :, CHUNK // 2 - 1:CHUNK // 2, :]
    a = jnp.einsum('bhntk,bhnsk->bhnts', qc * jnp.exp(b - b_mid), kc * jnp.exp(b_mid - b))
    causal = jnp.tril(jnp.ones((CHUNK, CHUNK), dtype=bool))
    a = jnp.where(causal, a, 0.0)
    o_intra = jnp.einsum('bhnts,bhnsv->bhntv', a, vc)
    q_in = qc * jnp.exp(b)
    k_st = kc * jnp.exp(b_last - b)
    d_last = jnp.exp(b_last[:, :, :, 0, :])

    def step(state, xs):
        qn, kn, vn, dn = xs
        o = jnp.einsum('bhtk,bhkv->bhtv', qn, state)
        state = dn[..., None] * state + jnp.einsum('bhtk,bhtv->bhkv', kn, vn)
        return state, o

    xs = (jnp.moveaxis(q_in, 2, 0), jnp.moveaxis(k_st, 2, 0),
          jnp.moveaxis(vc, 2, 0), jnp.moveaxis(d_last, 2, 0))
    s0 = jnp.zeros((B, H, DK, DV), jnp.float32)
    _, o_inter = lax.scan(step, s0, xs)
    o = o_intra + jnp.moveaxis(o_inter, 0, 2)
    return o.transpose(0, 2, 3, 1, 4).reshape(B, S, H, DV)


def _diff_attention(q, k, v, bias_table, lam):
    B, S, H = q.shape[0], q.shape[1], q.shape[2]
    nb = S // Q_BLOCK
    scale = DIFF_DH ** -0.5
    qb = q.reshape(B, nb, Q_BLOCK, H, 2, DIFF_DH).transpose(1, 0, 3, 4, 2, 5)
    kt = k.transpose(0, 2, 3, 1, 4)
    vt = v.transpose(0, 2, 1, 3)
    k_pos = jnp.arange(S, dtype=jnp.int32)

    def block(args):
        qblk, j = args
        q_pos = j * Q_BLOCK + jnp.arange(Q_BLOCK, dtype=jnp.int32)
        rel = q_pos[:, None] - k_pos[None, :]
        bias = bias_table[_t5_bucket(jnp.maximum(rel, 0))]
        bias = jnp.transpose(bias, (2, 0, 1)).astype(jnp.float32)
        s = jnp.einsum('bhmqd,bhmkd->bhmqk', qblk, kt,
                       preferred_element_type=jnp.float32) * scale
        s = s + bias[None, :, None]
        s = jnp.where((rel >= 0)[None, None, None], s, -jnp.inf)
        p = jax.nn.softmax(s, axis=-1)
        attn = p[:, :, 0] - lam * p[:, :, 1]
        return jnp.einsum('bhqk,bhkv->bhqv', attn.astype(v.dtype), vt)

    o = lax.map(block, (qb, jnp.arange(nb, dtype=jnp.int32)))
    return o.transpose(1, 0, 3, 2, 4).reshape(B, S, H, DIFF_DV)


def setup_inputs(seed: int = 0) -> dict:
    key = jax.random.key(seed)
    ks = jax.random.split(key, 24)
    beta = (8.0 * DEPTH) ** -0.25
    n = jax.random.normal
    f32 = jnp.float32
    return {
        "x": n(ks[0], (BATCH, SEQ, D_MODEL), f32),
        "c": n(ks[1], (BATCH, D_MODEL), f32),
        "w_ada": n(ks[2], (DEPTH, D_MODEL, 6 * D_MODEL), f32) * (0.5 * D_MODEL ** -0.5),
        "b_ada": n(ks[3], (DEPTH, 6 * D_MODEL), f32) * 0.01,
        "w_in": n(ks[4], (DEPTH, D_MODEL, IN_WIDTH), f32) * D_MODEL ** -0.5,
        "lb_logits": n(ks[5], (DEPTH + 1, HGRN_WIDTH), f32) * 0.1,
        "hgrn_norm_w": 1.0 + 0.01 * n(ks[6], (DEPTH, HGRN_DV), f32),
        "lam_q1": n(ks[7], (DEPTH, DIFF_DH), f32) * 0.1,
        "lam_k1": n(ks[8], (DEPTH, DIFF_DH), f32) * 0.1,
        "lam_q2": n(ks[9], (DEPTH, DIFF_DH), f32) * 0.1,
        "lam_k2": n(ks[10], (DEPTH, DIFF_DH), f32) * 0.1,
        "diff_norm_w": 1.0 + 0.01 * n(ks[11], (DEPTH, DIFF_DV), f32),
        "rel_bias": n(ks[12], (N_BUCKETS, DIFF_HEADS), f32) * 0.5,
        "w_out": n(ks[13], (DEPTH, MIX_WIDTH, D_MODEL), f32) * (beta * MIX_WIDTH ** -0.5),
        "ln1_g": 1.0 + 0.01 * n(ks[14], (DEPTH, D_MODEL), f32),
        "ln1_b": 0.01 * n(ks[15], (DEPTH, D_MODEL), f32),
        "w_gate": n(ks[16], (DEPTH, D_MODEL, D_FF), f32) * D_MODEL ** -0.5,
        "w_up": n(ks[17], (DEPTH, D_MODEL, D_FF), f32) * D_MODEL ** -0.5,
        "w_down": n(ks[18], (DEPTH, D_FF, D_MODEL), f32) * (beta * D_FF ** -0.5),
        "ln2_g": 1.0 + 0.01 * n(ks[19], (DEPTH, D_MODEL), f32),
        "ln2_b": 0.01 * n(ks[20], (DEPTH, D_MODEL), f32),
    }


def reference(x, c, w_ada, b_ada, w_in, lb_logits, hgrn_norm_w, lam_q1, lam_k1, lam_q2, lam_k2,
              diff_norm_w, rel_bias, w_out, ln1_g, ln1_b, w_gate, w_up, w_down, ln2_g, ln2_b):
    B, S, _ = x.shape
    alpha = (2.0 * DEPTH) ** 0.25
    lb_all = jnp.cumsum(jax.nn.softmax(lb_logits.astype(jnp.float32), axis=0), axis=0)
    splits = [HGRN_WIDTH, 2 * HGRN_WIDTH, 3 * HGRN_WIDTH, 4 * HGRN_WIDTH,
              4 * HGRN_WIDTH + DIFF_WIDTH, 4 * HGRN_WIDTH + 2 * DIFF_WIDTH]
    for l in range(DEPTH):
        ada = jax.nn.silu(c) @ w_ada[l] + b_ada[l]
        sh_m, sc_m, g_m, sh_f, sc_f, g_f = jnp.split(ada[:, None, :], 6, axis=-1)

        u = _layer_norm(x) * (1.0 + sc_m) + sh_m
        proj = u @ w_in[l]
        hq, hf, hi, hg, dq, dk, dv = jnp.split(proj, splits, axis=-1)

        lb = lb_all[l].reshape(HGRN_HEADS, HGRN_DK)
        o_h = _hgrn2_chunkwise(hq.reshape(B, S, HGRN_HEADS, HGRN_DK),
                               hf.reshape(B, S, HGRN_HEADS, HGRN_DK),
                               hi.reshape(B, S, HGRN_HEADS, HGRN_DV), lb).astype(x.dtype)
        o_h = _rms_norm(o_h, hgrn_norm_w[l]) * jax.nn.silu(hg.reshape(B, S, HGRN_HEADS, HGRN_DV))
        o_h = o_h.reshape(B, S, HGRN_WIDTH)

        lam_init = 0.8 - 0.6 * math.exp(-0.3 * l)
        lam = (jnp.exp(jnp.sum(lam_q1[l].astype(jnp.float32) * lam_k1[l].astype(jnp.float32)))
               - jnp.exp(jnp.sum(lam_q2[l].astype(jnp.float32) * lam_k2[l].astype(jnp.float32)))
               + lam_init)
        o_d = _diff_attention(dq.reshape(B, S, DIFF_HEADS, 2, DIFF_DH),
                              dk.reshape(B, S, DIFF_HEADS, 2, DIFF_DH),
                              dv.reshape(B, S, DIFF_HEADS, DIFF_DV), rel_bias, lam)
        o_d = (_rms_norm(o_d, diff_norm_w[l]) * (1.0 - lam_init)).reshape(B, S, DIFF_WIDTH)

        mix = jnp.concatenate([o_h, o_d], axis=-1) @ w_out[l]
        x = _layer_norm(alpha * x + (1.0 + g_m) * mix) * ln1_g[l] + ln1_b[l]

        u = _layer_norm(x) * (1.0 + sc_f) + sh_f
        y = (jax.nn.silu(u @ w_gate[l]) * (u @ w_up[l])) @ w_down[l]
        x = _layer_norm(alpha * x + (1.0 + g_f) * y) * ln2_g[l] + ln2_b[l]
    return x
```

```python
import functools
import math

import jax
import jax.numpy as jnp
import numpy as np
from jax import lax
from jax.experimental import pallas as pl
from jax.experimental.pallas import tpu as pltpu

HGRN_HEADS = 4
HGRN_DK = 128
HGRN_DV = 128
HGRN_WIDTH = HGRN_HEADS * HGRN_DV
DIFF_HEADS = 4
DIFF_DH = 64
DIFF_DV = 2 * DIFF_DH
DIFF_WIDTH = DIFF_HEADS * DIFF_DV
N_BUCKETS = 32
MAX_DISTANCE = 128
CHUNK = 64
LN_EPS = 1e-5
RMS_EPS = 1e-6

LANES = 128
V7X_VMEM_BYTES = 64 * 1024 * 1024

ROW_TILE = 512
HGRN_TILE = 512
ATT_BLOCK = 256
FF_CHUNK = 256
PROJ_CHUNK = 512
MASK_VALUE = -1e30

BF16 = jnp.bfloat16
F32 = jnp.float32


def _vmem_limit(block_bytes):
    return int(min(block_bytes, V7X_VMEM_BYTES - 4 * 1024 * 1024))


def _layer_norm_rows(x):
    mu = jnp.mean(x, axis=-1, keepdims=True)
    xc = x - mu
    var = jnp.mean(xc * xc, axis=-1, keepdims=True)
    return xc * lax.rsqrt(var + LN_EPS)


def _sigmoid_pair(z):
    t = jnp.exp(-jnp.abs(z))
    r = 1.0 / (1.0 + t)
    tr = t * r
    pos = z >= 0
    return jnp.where(pos, r, tr), jnp.where(pos, tr, r)


def _dot_nt(a, b):
    return lax.dot_general(a, b, (((1,), (1,)), ((), ())), preferred_element_type=F32)


def _dot_tn(a, b):
    return lax.dot_general(a, b, (((0,), (0,)), ((), ())), preferred_element_type=F32)


def _prep_kernel(c_ref, w_ref, b_ref, lbl_ref, q1_ref, k1_ref, q2_ref, k2_ref,
                 ada_ref, lb_ref, lam_ref, *, layer, lam_init):
    c = c_ref[...]
    sc = c * (1.0 / (1.0 + jnp.exp(-c)))
    ada_ref[...] = jnp.dot(sc.astype(BF16), w_ref[...].astype(BF16),
                           preferred_element_type=F32) + b_ref[...]

    @pl.when(pl.program_id(0) == 0)
    def _():
        n_rows = lbl_ref.shape[0]
        rows = [lbl_ref[r:r + 1, :] for r in range(n_rows)]
        mx = functools.reduce(jnp.maximum, rows)
        es = [jnp.exp(r - mx) for r in rows]
        den = functools.reduce(lambda a, b: a + b, es)
        num = functools.reduce(lambda a, b: a + b, es[:layer + 1])
        lb_ref[...] = num / den
        s1 = jnp.sum(q1_ref[...] * k1_ref[...], axis=-1, keepdims=True)
        s2 = jnp.sum(q2_ref[...] * k2_ref[...], axis=-1, keepdims=True)
        lam = jnp.exp(s1) - jnp.exp(s2) + lam_init
        lam_ref[...] = jnp.broadcast_to(lam, lam_ref.shape)


def _prep(c, w_ada, b_ada, lb_logits, lam_q1, lam_k1, lam_q2, lam_k2, *, layer, lam_init):
    B, D = c.shape
    n_out = w_ada.shape[1]
    col = D
    width = lb_logits.shape[1]
    small = lambda shape: pl.BlockSpec(shape, lambda j: (0,) * len(shape))
    return pl.pallas_call(
        functools.partial(_prep_kernel, layer=layer, lam_init=lam_init),
        grid=(n_out // col,),
        in_specs=[
            small((B, D)),
            pl.BlockSpec((D, col), lambda j: (0, j)),
            pl.BlockSpec((1, col), lambda j: (0, j)),
            small(lb_logits.shape),
            small((1, DIFF_DH)), small((1, DIFF_DH)), small((1, DIFF_DH)), small((1, DIFF_DH)),
        ],
        out_specs=[
            pl.BlockSpec((B, col), lambda j: (0, j)),
            small((1, width)),
            small((1, LANES)),
        ],
        out_shape=[
            jax.ShapeDtypeStruct((B, n_out), F32),
            jax.ShapeDtypeStruct((1, width), F32),
            jax.ShapeDtypeStruct((1, LANES), F32),
        ],
        compiler_params=pltpu.CompilerParams(
            dimension_semantics=("arbitrary",),
            vmem_limit_bytes=_vmem_limit(4 * D * col * 4 + (8 << 20))),
        name="prep",
    )(c, w_ada, b_ada, lb_logits, lam_q1, lam_k1, lam_q2, lam_k2)


def _t5_bucket_np(dist):
    max_exact = N_BUCKETS // 2
    d = np.maximum(dist, 1).astype(np.float32)
    large = max_exact + (np.log(d / max_exact) / math.log(MAX_DISTANCE / max_exact)
                         * (N_BUCKETS - max_exact)).astype(np.int32)
    large = np.minimum(large, N_BUCKETS - 1)
    return np.where(dist < max_exact, dist, large)


def _num_near_tiles(block):
    first_last = int(np.argmax(_t5_bucket_np(np.arange(0, 4 * MAX_DISTANCE)) == N_BUCKETS - 1))
    assert np.all(_t5_bucket_np(np.arange(first_last, 64 * MAX_DISTANCE)) == N_BUCKETS - 1)
    t = 0
    while t * block - (block - 1) < first_last:
        t += 1
    return t


def _bias_kernel(tab_ref, o_ref, *, block):
    h = pl.program_id(0)
    t = pl.program_id(1)
    r = lax.broadcasted_iota(jnp.int32, (block, block), 0)
    c = lax.broadcasted_iota(jnp.int32, (block, block), 1)
    dist = r - c + t * block
    rel = jnp.maximum(dist, 0)
    max_exact = N_BUCKETS // 2
    d = jnp.maximum(rel, 1).astype(F32)
    large = max_exact + (jnp.log(d / max_exact) / math.log(MAX_DISTANCE / max_exact)
                         * (N_BUCKETS - max_exact)).astype(jnp.int32)
    large = jnp.minimum(large, N_BUCKETS - 1)
    bucket = jnp.where(rel < max_exact, rel, large)
    bias = jnp.zeros((block, block), F32)
    for b in range(N_BUCKETS):
        bias = jnp.where(bucket == b, tab_ref[b, h], bias)
    o_ref[0, 0] = jnp.where(dist >= 0, bias, MASK_VALUE)


def _bias_tiles(rel_bias, *, block, n_near):
    H = rel_bias.shape[1]
    return pl.pallas_call(
        functools.partial(_bias_kernel, block=block),
        grid=(H, n_near),
        in_specs=[pl.BlockSpec(memory_space=pltpu.SMEM)],
        out_specs=pl.BlockSpec((1, 1, block, block), lambda h, t: (h, t, 0, 0)),
        out_shape=jax.ShapeDtypeStruct((H, n_near, block, block), F32),
        compiler_params=pltpu.CompilerParams(dimension_semantics=("arbitrary", "arbitrary")),
        name="bias_tiles",
    )(rel_bias)


def _in_proj_kernel(x_ref, ada_ref, w_ref, o_ref):
    x = x_ref[...]
    shift = ada_ref[0, 0:1, :]
    scale = ada_ref[0, 1:2, :]
    u = (_layer_norm_rows(x) * (1.0 + scale) + shift).astype(BF16)
    n_out = o_ref.shape[1]
    for c0 in range(0, n_out, PROJ_CHUNK):
        o_ref[:, c0:c0 + PROJ_CHUNK] = jnp.dot(
            u, w_ref[:, c0:c0 + PROJ_CHUNK], preferred_element_type=F32).astype(o_ref.dtype)


def _in_proj(x2, ada3, w_in, *, seq):
    N, D = x2.shape
    n_out = w_in.shape[1]
    tm = ROW_TILE
    per_batch = seq // tm
    vmem = (2 * tm * D * 4 + D * n_out * 2 + 2 * tm * n_out * 2 + 2 * 8 * D * 4
            + 3 * tm * D * 4 + 2 * tm * PROJ_CHUNK * 4 + (4 << 20))
    return pl.pallas_call(
        _in_proj_kernel,
        grid=(N // tm,),
        in_specs=[
            pl.BlockSpec((tm, D), lambda i: (i, 0)),
            pl.BlockSpec((1, ada3.shape[1], D), lambda i: (i // per_batch, 0, 0)),
            pl.BlockSpec((D, n_out), lambda i: (0, 0), pipeline_mode=pl.Buffered(1)),
        ],
        out_specs=pl.BlockSpec((tm, n_out), lambda i: (i, 0)),
        out_shape=jax.ShapeDtypeStruct((N, n_out), BF16),
        compiler_params=pltpu.CompilerParams(
            dimension_semantics=("arbitrary",), vmem_limit_bytes=_vmem_limit(vmem)),
        name="in_proj",
    )(x2, ada3, w_in)


def _chunk_cumsum(x):
    n = x.shape[0]
    row = lax.broadcasted_iota(jnp.int32, x.shape, 0)
    k = 1
    while k < n:
        x = x + jnp.where(row >= k, pltpu.roll(x, k, 0), 0.0)
        k *= 2
    return x


def _hgrn_kernel(q_ref, f_ref, i_ref, g_ref, lb_ref, nw_ref, o_ref, st_ref):
    @pl.when(pl.program_id(1) == 0)
    def _():
        st_ref[...] = jnp.zeros_like(st_ref)

    lb = lb_ref[...]
    one_m_lb = 1.0 - lb
    nw = nw_ref[...]
    n_chunks = q_ref.shape[1] // CHUNK
    tri_r = lax.broadcasted_iota(jnp.int32, (CHUNK, CHUNK), 0)
    tri_c = lax.broadcasted_iota(jnp.int32, (CHUNK, CHUNK), 1)
    causal = tri_r >= tri_c

    def chunk(n, carry):
        rows = pl.ds(pl.multiple_of(n * CHUNK, CHUNK), CHUNK)
        fp = f_ref[0, rows, :].astype(F32)
        q = q_ref[0, rows, :].astype(F32) * (HGRN_DK ** -0.5)
        v = i_ref[0, rows, :]
        g = g_ref[0, rows, :].astype(F32)
        sig, nsig = _sigmoid_pair(fp)
        log_f = jnp.log(lb + one_m_lb * sig)
        kc = one_m_lb * nsig
        b = _chunk_cumsum(log_f)
        b_mid = b[CHUNK // 2 - 1:CHUNK // 2, :]
        b_last = b[CHUNK - 1:CHUNK, :]
        qd = (q * jnp.exp(b - b_mid)).astype(BF16)
        kd = (kc * jnp.exp(b_mid - b)).astype(BF16)
        q_in = (q * jnp.exp(b)).astype(BF16)
        k_st = (kc * jnp.exp(b_last - b)).astype(BF16)
        d_last = jnp.exp(b_last)
        gate = g * (1.0 / (1.0 + jnp.exp(-g)))
        for h in range(HGRN_HEADS):
            sl = slice(h * HGRN_DK, (h + 1) * HGRN_DK)
            a = jnp.where(causal, _dot_nt(qd[:, sl], kd[:, sl]), 0.0)
            st = st_ref[h]
            o = (jnp.dot(a.astype(BF16), v[:, sl], preferred_element_type=F32)
                 + _dot_nt(q_in[:, sl], st.astype(BF16)))
            st_ref[h] = st * d_last[:, sl] + _dot_tn(v[:, sl], k_st[:, sl])
            ms = jnp.mean(o * o, axis=-1, keepdims=True)
            y = o * lax.rsqrt(ms + RMS_EPS) * nw * gate[:, sl]
            o_ref[0, rows, sl] = y.astype(o_ref.dtype)
        return carry

    lax.fori_loop(0, n_chunks, chunk, 0)


def _hgrn(proj3, lb, norm_w):
    B, S, _ = proj3.shape
    ts = HGRN_TILE
    W = HGRN_WIDTH
    spec = lambda j: pl.BlockSpec((1, ts, W), lambda b, s, j=j: (b, s, j))
    vmem = 5 * 2 * ts * W * 2 + HGRN_HEADS * HGRN_DK * HGRN_DV * 4 + (16 << 20)
    return pl.pallas_call(
        _hgrn_kernel,
        grid=(B, S // ts),
        in_specs=[spec(0), spec(1), spec(2), spec(3),
                  pl.BlockSpec((1, W), lambda b, s: (0, 0)),
                  pl.BlockSpec((1, HGRN_DV), lambda b, s: (0, 0))],
        out_specs=pl.BlockSpec((1, ts, W), lambda b, s: (b, s, 0)),
        out_shape=jax.ShapeDtypeStruct((B, S, W), BF16),
        scratch_shapes=[pltpu.VMEM((HGRN_HEADS, HGRN_DV, HGRN_DK), F32)],
        compiler_params=pltpu.CompilerParams(
            dimension_semantics=("arbitrary", "arbitrary"), vmem_limit_bytes=_vmem_limit(vmem)),
        name="hgrn",
    )(proj3, proj3, proj3, proj3, lb, norm_w)


def _attn_kernel(tab_ref, q_ref, k_ref, v_ref, bias_ref, lam_ref, nw_ref, o_ref,
                 m_ref, l_ref, acc_ref, *, block, n_near, out_scale):
    h = pl.program_id(1)
    qi = pl.program_id(2)
    tq = block

    q12 = q_ref[0]
    lane = lax.broadcasted_iota(jnp.int32, q12.shape, 1)
    zero = jnp.zeros_like(q12)
    scale = jnp.asarray(DIFF_DH ** -0.5, q12.dtype)
    qz = jnp.concatenate([jnp.where(lane < DIFF_DH, q12, zero),
                          jnp.where(lane >= DIFF_DH, q12, zero)], axis=0) * scale

    m_ref[...] = jnp.full_like(m_ref, MASK_VALUE)
    l_ref[...] = jnp.zeros_like(l_ref)
    acc_ref[...] = jnp.zeros_like(acc_ref)

    def step(kj, bias_tile, bias_const):
        rows = pl.ds(pl.multiple_of(kj * block, block), block)
        k = k_ref[0, rows, :]
        v = v_ref[0, rows, :]
        s = _dot_nt(qz, k)
        if bias_tile is not None:
            s = s + jnp.concatenate([bias_tile, bias_tile], axis=0)
        m_prev = m_ref[...]
        m_cur = jnp.max(s, axis=-1, keepdims=True) + bias_const
        m_new = jnp.maximum(m_prev, m_cur)
        p = jnp.exp(s - (m_new[:, :1] - bias_const))
        alpha = jnp.exp(m_prev - m_new)
        l_ref[...] = alpha * l_ref[...] + jnp.sum(p, axis=-1, keepdims=True)
        acc_ref[...] = alpha * acc_ref[...] + jnp.dot(p.astype(v.dtype), v,
                                                     preferred_element_type=F32)
        m_ref[...] = m_new

    far_bias = tab_ref[N_BUCKETS - 1, h]

    def far_body(kj, carry):
        step(kj, None, far_bias)
        return carry

    lax.fori_loop(0, jnp.maximum(qi - (n_near - 1), 0), far_body, 0)

    for t in range(n_near - 1, 0, -1):
        @pl.when(qi >= t)
        def _(t=t):
            step(qi - t, bias_ref[0, t], 0.0)

    step(qi, bias_ref[0, 0], 0.0)

    inv_l = 1.0 / l_ref[...]
    o_all = acc_ref[...] * inv_l
    o = o_all[:tq] - lam_ref[...] * o_all[tq:]
    ms = jnp.mean(o * o, axis=-1, keepdims=True)
    y = o * lax.rsqrt(ms + RMS_EPS) * nw_ref[...] * out_scale
    o_ref[0] = y.astype(o_ref.dtype)


def _attn(proj3, bias_tiles, rel_bias, lam, norm_w, *, out_scale):
    B, S, _ = proj3.shape
    H = DIFF_HEADS
    block = ATT_BLOCK
    n_near = bias_tiles.shape[1]
    q_blk = 4 * HGRN_HEADS
    k_blk = q_blk + H
    v_blk = k_blk + H
    vmem = (2 * 2 * S * LANES * 2 + 2 * n_near * block * block * 4
            + 3 * 2 * block * LANES * 4 + 6 * 2 * block * block * 4 + (8 << 20))
    return pl.pallas_call(
        functools.partial(_attn_kernel, block=block, n_near=n_near, out_scale=out_scale),
        grid=(B, H, S // block),
        in_specs=[
            pl.BlockSpec(memory_space=pltpu.SMEM),
            pl.BlockSpec((1, block, LANES), lambda b, h, i: (b, i, q_blk + h)),
            pl.BlockSpec((1, S, LANES), lambda b, h, i: (b, 0, k_blk + h)),
            pl.BlockSpec((1, S, LANES), lambda b, h, i: (b, 0, v_blk + h)),
            pl.BlockSpec((1, n_near, block, block), lambda b, h, i: (h, 0, 0, 0)),
            pl.BlockSpec((1, LANES), lambda b, h, i: (0, 0)),
            pl.BlockSpec((1, DIFF_DV), lambda b, h, i: (0, 0)),
        ],
        out_specs=pl.BlockSpec((1, block, LANES), lambda b, h, i: (b, i, h)),
        out_shape=jax.ShapeDtypeStruct((B, S, DIFF_WIDTH), BF16),
        scratch_shapes=[pltpu.VMEM((2 * block, LANES), F32),
                        pltpu.VMEM((2 * block, LANES), F32),
                        pltpu.VMEM((2 * block, DIFF_DV), F32)],
        compiler_params=pltpu.CompilerParams(
            dimension_semantics=("arbitrary", "arbitrary", "arbitrary"),
            vmem_limit_bytes=_vmem_limit(vmem)),
        name="attn",
    )(rel_bias, proj3, proj3, proj3, bias_tiles, lam, norm_w)


def _out_ffn_kernel(x_ref, oh_ref, od_ref, ada_ref, wo_ref, g1_ref, b1_ref,
                    wg_ref, wu_ref, wd_ref, g2_ref, b2_ref, o_ref, *, alpha):
    gate_m = ada_ref[0, 2:3, :]
    shift_f = ada_ref[0, 3:4, :]
    scale_f = ada_ref[0, 4:5, :]
    gate_f = ada_ref[0, 5:6, :]
    hw = oh_ref.shape[1]
    mix = (jnp.dot(oh_ref[...], wo_ref[:hw, :], preferred_element_type=F32)
           + jnp.dot(od_ref[...], wo_ref[hw:, :], preferred_element_type=F32))
    x1 = _layer_norm_rows(alpha * x_ref[...] + (1.0 + gate_m) * mix) * g1_ref[...] + b1_ref[...]
    u = (_layer_norm_rows(x1) * (1.0 + scale_f) + shift_f).astype(BF16)
    d_ff = wg_ref.shape[1]
    y = jnp.zeros(x1.shape, F32)
    for c0 in range(0, d_ff, FF_CHUNK):
        a = jnp.dot(u, wg_ref[:, c0:c0 + FF_CHUNK], preferred_element_type=F32)
        b = jnp.dot(u, wu_ref[:, c0:c0 + FF_CHUNK], preferred_element_type=F32)
        hid = (a * (1.0 / (1.0 + jnp.exp(-a))) * b).astype(BF16)
        y = y + jnp.dot(hid, wd_ref[c0:c0 + FF_CHUNK, :], preferred_element_type=F32)
    o_ref[...] = _layer_norm_rows(alpha * x1 + (1.0 + gate_f) * y) * g2_ref[...] + b2_ref[...]


def _out_ffn(x2, oh2, od2, ada3, w_out, ln1_g, ln1_b, w_gate, w_up, w_down, ln2_g, ln2_b,
             *, seq, alpha):
    N, D = x2.shape
    d_ff = w_gate.shape[1]
    tm = ROW_TILE
    per_batch = seq // tm
    row = lambda w: pl.BlockSpec((tm, w), lambda i: (i, 0))
    const = lambda shape: pl.BlockSpec(shape, lambda i: (0, 0), pipeline_mode=pl.Buffered(1))
    vec = pl.BlockSpec((1, D), lambda i: (0, 0))
    weights = (w_out.size + w_gate.size + w_up.size + w_down.size) * 2
    vmem = (weights + 2 * 2 * tm * D * 4 + 2 * 2 * tm * oh2.shape[1] * 2
            + 6 * tm * D * 4 + 4 * tm * FF_CHUNK * 4 + (6 << 20))
    return pl.pallas_call(
        functools.partial(_out_ffn_kernel, alpha=alpha),
        grid=(N // tm,),
        in_specs=[
            row(D), row(oh2.shape[1]), row(od2.shape[1]),
            pl.BlockSpec((1, ada3.shape[1], D), lambda i: (i // per_batch, 0, 0)),
            const(w_out.shape), vec, vec,
            const(w_gate.shape), const(w_up.shape), const(w_down.shape), vec, vec,
        ],
        out_specs=row(D),
        out_shape=jax.ShapeDtypeStruct((N, D), F32),
        compiler_params=pltpu.CompilerParams(
            dimension_semantics=("arbitrary",), vmem_limit_bytes=_vmem_limit(vmem)),
        name="out_ffn",
    )(x2, oh2, od2, ada3, w_out, ln1_g, ln1_b, w_gate, w_up, w_down, ln2_g, ln2_b)


def kernel(x, c, w_ada, b_ada, w_in, lb_logits, hgrn_norm_w, lam_q1, lam_k1, lam_q2, lam_k2,
           diff_norm_w, rel_bias, w_out, ln1_g, ln1_b, w_gate, w_up, w_down, ln2_g, ln2_b):
    B, S, D = x.shape
    depth = w_ada.shape[0]
    assert S % ROW_TILE == 0 and S % HGRN_TILE == 0 and S % ATT_BLOCK == 0
    assert HGRN_TILE % CHUNK == 0 and w_in.shape[2] % PROJ_CHUNK == 0
    assert w_gate.shape[2] % FF_CHUNK == 0
    alpha = (2.0 * depth) ** 0.25
    n_near = _num_near_tiles(ATT_BLOCK)
    bias_tiles = _bias_tiles(rel_bias, block=ATT_BLOCK, n_near=n_near)
    for l in range(depth):
        lam_init = 0.8 - 0.6 * math.exp(-0.3 * l)
        ada, lb, lam = _prep(c, w_ada[l], b_ada[l][None, :], lb_logits,
                             lam_q1[l][None, :], lam_k1[l][None, :],
                             lam_q2[l][None, :], lam_k2[l][None, :],
                             layer=l, lam_init=lam_init)
        ada3 = ada.reshape(B, 6, D)
        x2 = x.reshape(B * S, D)
        proj = _in_proj(x2, ada3, w_in[l].astype(BF16), seq=S)
        proj3 = proj.reshape(B, S, proj.shape[1])
        o_h = _hgrn(proj3, lb, hgrn_norm_w[l][None, :])
        o_d = _attn(proj3, bias_tiles, rel_bias, lam, diff_norm_w[l][None, :],
                    out_scale=1.0 - lam_init)
        out = _out_ffn(x2, o_h.reshape(B * S, -1), o_d.reshape(B * S, -1), ada3,
                       w_out[l].astype(BF16), ln1_g[l][None, :], ln1_b[l][None, :],
                       w_gate[l].astype(BF16), w_up[l].astype(BF16), w_down[l].astype(BF16),
                       ln2_g[l][None, :], ln2_b[l][None, :], seq=S, alpha=alpha)
        x = out.reshape(B, S, D)
    return x
```

```python
import functools
import math

import jax
import jax.numpy as jnp
import numpy as np
from jax import lax
from jax.experimental import pallas as pl
from jax.experimental.pallas import tpu as pltpu

HGRN_HEADS = 4
HGRN_DK = 128
HGRN_DV = 128
HGRN_WIDTH = HGRN_HEADS * HGRN_DV
DIFF_HEADS = 4
DIFF_DH = 64
DIFF_DV = 2 * DIFF_DH
DIFF_WIDTH = DIFF_HEADS * DIFF_DV
N_BUCKETS = 32
MAX_DISTANCE = 128
CHUNK = 64
LN_EPS = 1e-5
RMS_EPS = 1e-6
LOG2E = math.log2(math.e)

LANES = 128
V7X_VMEM_BYTES = 64 * 1024 * 1024

ROW_TILE = 512
HGRN_TILE = 512
ATT_BLOCK = 512
FF_CHUNK = 256
PROJ_CHUNK = 512
MASK_VALUE = -1e30

BF16 = jnp.bfloat16
F32 = jnp.float32


def _vmem_limit(block_bytes):
    return int(min(block_bytes, V7X_VMEM_BYTES - 4 * 1024 * 1024))


def _layer_norm_rows(x):
    mu = jnp.mean(x, axis=-1, keepdims=True)
    xc = x - mu
    var = jnp.mean(xc * xc, axis=-1, keepdims=True)
    return xc * lax.rsqrt(var + LN_EPS)


def _sigmoid_pair(z):
    t = jnp.exp(-jnp.abs(z))
    r = 1.0 / (1.0 + t)
    tr = t * r
    pos = z >= 0
    return jnp.where(pos, r, tr), jnp.where(pos, tr, r)


def _dot_nt(a, b):
    return lax.dot_general(a, b, (((1,), (1,)), ((), ())), preferred_element_type=F32)


def _dot_tn(a, b):
    return lax.dot_general(a, b, (((0,), (0,)), ((), ())), preferred_element_type=F32)


def _prep_kernel(c_ref, w_ref, b_ref, lbl_ref, q1_ref, k1_ref, q2_ref, k2_ref,
                 ada_ref, lb_ref, lam_ref, *, layer, lam_init):
    c = c_ref[...]
    sc = c * (1.0 / (1.0 + jnp.exp(-c)))
    ada_ref[...] = jnp.dot(sc.astype(BF16), w_ref[...].astype(BF16),
                           preferred_element_type=F32) + b_ref[...]

    @pl.when(pl.program_id(0) == 0)
    def _():
        n_rows = lbl_ref.shape[0]
        rows = [lbl_ref[r:r + 1, :] for r in range(n_rows)]
        mx = functools.reduce(jnp.maximum, rows)
        es = [jnp.exp(r - mx) for r in rows]
        den = functools.reduce(lambda a, b: a + b, es)
        num = functools.reduce(lambda a, b: a + b, es[:layer + 1])
        lb_ref[...] = num / den
        s1 = jnp.sum(q1_ref[...] * k1_ref[...], axis=-1, keepdims=True)
        s2 = jnp.sum(q2_ref[...] * k2_ref[...], axis=-1, keepdims=True)
        lam = jnp.exp(s1) - jnp.exp(s2) + lam_init
        lam_ref[...] = jnp.broadcast_to(lam, lam_ref.shape)


def _prep(c, w_ada, b_ada, lb_logits, lam_q1, lam_k1, lam_q2, lam_k2, *, layer, lam_init):
    B, D = c.shape
    n_out = w_ada.shape[1]
    col = D
    width = lb_logits.shape[1]
    small = lambda shape: pl.BlockSpec(shape, lambda j: (0,) * len(shape))
    return pl.pallas_call(
        functools.partial(_prep_kernel, layer=layer, lam_init=lam_init),
        grid=(n_out // col,),
        in_specs=[
            small((B, D)),
            pl.BlockSpec((D, col), lambda j: (0, j)),
            pl.BlockSpec((1, col), lambda j: (0, j)),
            small(lb_logits.shape),
            small((1, DIFF_DH)), small((1, DIFF_DH)), small((1, DIFF_DH)), small((1, DIFF_DH)),
        ],
        out_specs=[
            pl.BlockSpec((B, col), lambda j: (0, j)),
            small((1, width)),
            small((1, LANES)),
        ],
        out_shape=[
            jax.ShapeDtypeStruct((B, n_out), F32),
            jax.ShapeDtypeStruct((1, width), F32),
            jax.ShapeDtypeStruct((1, LANES), F32),
        ],
        compiler_params=pltpu.CompilerParams(
            dimension_semantics=("arbitrary",),
            vmem_limit_bytes=_vmem_limit(4 * D * col * 4 + (8 << 20))),
        name="prep",
    )(c, w_ada, b_ada, lb_logits, lam_q1, lam_k1, lam_q2, lam_k2)


def _t5_bucket_np(dist):
    max_exact = N_BUCKETS // 2
    d = np.maximum(dist, 1).astype(np.float32)
    large = max_exact + (np.log(d / max_exact) / math.log(MAX_DISTANCE / max_exact)
                         * (N_BUCKETS - max_exact)).astype(np.int32)
    large = np.minimum(large, N_BUCKETS - 1)
    return np.where(dist < max_exact, dist, large)


def _num_near_tiles(block):
    first_last = int(np.argmax(_t5_bucket_np(np.arange(0, 4 * MAX_DISTANCE)) == N_BUCKETS - 1))
    assert np.all(_t5_bucket_np(np.arange(first_last, 64 * MAX_DISTANCE)) == N_BUCKETS - 1)
    t = 0
    while t * block - (block - 1) < first_last:
        t += 1
    return t


def _bias_kernel(tab_ref, o_ref, *, block):
    h = pl.program_id(0)
    t = pl.program_id(1)
    r = lax.broadcasted_iota(jnp.int32, (block, block), 0)
    c = lax.broadcasted_iota(jnp.int32, (block, block), 1)
    dist = r - c + t * block
    rel = jnp.maximum(dist, 0)
    max_exact = N_BUCKETS // 2
    d = jnp.maximum(rel, 1).astype(F32)
    large = max_exact + (jnp.log(d / max_exact) / math.log(MAX_DISTANCE / max_exact)
                         * (N_BUCKETS - max_exact)).astype(jnp.int32)
    large = jnp.minimum(large, N_BUCKETS - 1)
    bucket = jnp.where(rel < max_exact, rel, large)
    bias = jnp.zeros((block, block), F32)
    for b in range(N_BUCKETS):
        bias = jnp.where(bucket == b, tab_ref[b, h], bias)
    o_ref[0, 0] = jnp.where(dist >= 0, bias * LOG2E, MASK_VALUE)


def _bias_tiles(rel_bias, *, block, n_near):
    H = rel_bias.shape[1]
    return pl.pallas_call(
        functools.partial(_bias_kernel, block=block),
        grid=(H, n_near),
        in_specs=[pl.BlockSpec(memory_space=pltpu.SMEM)],
        out_specs=pl.BlockSpec((1, 1, block, block), lambda h, t: (h, t, 0, 0)),
        out_shape=jax.ShapeDtypeStruct((H, n_near, block, block), F32),
        compiler_params=pltpu.CompilerParams(dimension_semantics=("arbitrary", "arbitrary")),
        name="bias_tiles",
    )(rel_bias)


def _in_proj_kernel(x_ref, ada_ref, w_ref, o_ref, *, q_cols, q_scale):
    x = x_ref[...]
    shift = ada_ref[0, 0:1, :]
    scale = ada_ref[0, 1:2, :]
    u = (_layer_norm_rows(x) * (1.0 + scale) + shift).astype(BF16)
    n_out = o_ref.shape[1]
    for c0 in range(0, n_out, PROJ_CHUNK):
        acc = jnp.dot(u, w_ref[:, c0:c0 + PROJ_CHUNK], preferred_element_type=F32)
        if q_cols[0] <= c0 < q_cols[1]:
            acc = acc * q_scale
        o_ref[:, c0:c0 + PROJ_CHUNK] = acc.astype(o_ref.dtype)


def _in_proj(x2, ada3, w_in, *, seq, q_cols, q_scale):
    N, D = x2.shape
    n_out = w_in.shape[1]
    tm = ROW_TILE
    per_batch = seq // tm
    assert q_cols[0] % PROJ_CHUNK == 0 and q_cols[1] % PROJ_CHUNK == 0
    vmem = (2 * tm * D * 4 + D * n_out * 2 + 2 * tm * n_out * 2 + 2 * 8 * D * 4
            + 3 * tm * D * 4 + 2 * tm * PROJ_CHUNK * 4 + (4 << 20))
    return pl.pallas_call(
        functools.partial(_in_proj_kernel, q_cols=q_cols, q_scale=q_scale),
        grid=(N // tm,),
        in_specs=[
            pl.BlockSpec((tm, D), lambda i: (i, 0)),
            pl.BlockSpec((1, ada3.shape[1], D), lambda i: (i // per_batch, 0, 0)),
            pl.BlockSpec((D, n_out), lambda i: (0, 0), pipeline_mode=pl.Buffered(1)),
        ],
        out_specs=pl.BlockSpec((tm, n_out), lambda i: (i, 0)),
        out_shape=jax.ShapeDtypeStruct((N, n_out), BF16),
        compiler_params=pltpu.CompilerParams(
            dimension_semantics=("arbitrary",), vmem_limit_bytes=_vmem_limit(vmem)),
        name="in_proj",
    )(x2, ada3, w_in)


def _chunk_cumsum(x):
    n = x.shape[0]
    row = lax.broadcasted_iota(jnp.int32, x.shape, 0)
    k = 1
    while k < n:
        x = x + jnp.where(row >= k, pltpu.roll(x, k, 0), 0.0)
        k *= 2
    return x


def _hgrn_kernel(q_ref, f_ref, i_ref, g_ref, lb_ref, nw_ref, o_ref, st_ref):
    @pl.when(pl.program_id(1) == 0)
    def _():
        st_ref[...] = jnp.zeros_like(st_ref)

    lb = lb_ref[...]
    one_m_lb = 1.0 - lb
    nw = nw_ref[...]
    n_chunks = q_ref.shape[1] // CHUNK
    tri_r = lax.broadcasted_iota(jnp.int32, (CHUNK, CHUNK), 0)
    tri_c = lax.broadcasted_iota(jnp.int32, (CHUNK, CHUNK), 1)
    causal = tri_r >= tri_c

    def chunk(n, carry):
        rows = pl.ds(pl.multiple_of(n * CHUNK, CHUNK), CHUNK)
        fp = f_ref[0, rows, :].astype(F32)
        q = q_ref[0, rows, :].astype(F32) * (HGRN_DK ** -0.5)
        v = i_ref[0, rows, :]
        g = g_ref[0, rows, :].astype(F32)
        sig, nsig = _sigmoid_pair(fp)
        log_f = jnp.log(lb + one_m_lb * sig)
        kc = one_m_lb * nsig
        b = _chunk_cumsum(log_f)
        b_mid = b[CHUNK // 2 - 1:CHUNK // 2, :]
        b_last = b[CHUNK - 1:CHUNK, :]
        qd = (q * jnp.exp(b - b_mid)).astype(BF16)
        kd = (kc * jnp.exp(b_mid - b)).astype(BF16)
        q_in = (q * jnp.exp(b)).astype(BF16)
        k_st = (kc * jnp.exp(b_last - b)).astype(BF16)
        d_last = jnp.exp(b_last)
        gate = g * (1.0 / (1.0 + jnp.exp(-g)))
        for h in range(HGRN_HEADS):
            sl = slice(h * HGRN_DK, (h + 1) * HGRN_DK)
            a = jnp.where(causal, _dot_nt(qd[:, sl], kd[:, sl]), 0.0)
            st = st_ref[h]
            o = (jnp.dot(a.astype(BF16), v[:, sl], preferred_element_type=F32)
                 + _dot_nt(q_in[:, sl], st.astype(BF16)))
            st_ref[h] = st * d_last[:, sl] + _dot_tn(v[:, sl], k_st[:, sl])
            ms = jnp.mean(o * o, axis=-1, keepdims=True)
            y = o * lax.rsqrt(ms + RMS_EPS) * nw * gate[:, sl]
            o_ref[0, rows, sl] = y.astype(o_ref.dtype)
        return carry

    lax.fori_loop(0, n_chunks, chunk, 0)


def _hgrn(proj3, lb, norm_w):
    B, S, _ = proj3.shape
    ts = HGRN_TILE
    W = HGRN_WIDTH
    spec = lambda j: pl.BlockSpec((1, ts, W), lambda b, s, j=j: (b, s, j))
    vmem = 5 * 2 * ts * W * 2 + HGRN_HEADS * HGRN_DK * HGRN_DV * 4 + (16 << 20)
    return pl.pallas_call(
        _hgrn_kernel,
        grid=(B, S // ts),
        in_specs=[spec(0), spec(1), spec(2), spec(3),
                  pl.BlockSpec((1, W), lambda b, s: (0, 0)),
                  pl.BlockSpec((1, HGRN_DV), lambda b, s: (0, 0))],
        out_specs=pl.BlockSpec((1, ts, W), lambda b, s: (b, s, 0)),
        out_shape=jax.ShapeDtypeStruct((B, S, W), BF16),
        scratch_shapes=[pltpu.VMEM((HGRN_HEADS, HGRN_DV, HGRN_DK), F32)],
        compiler_params=pltpu.CompilerParams(
            dimension_semantics=("arbitrary", "arbitrary"), vmem_limit_bytes=_vmem_limit(vmem)),
        name="hgrn",
    )(proj3, proj3, proj3, proj3, lb, norm_w)


def _attn_kernel(tab_ref, q_ref, k_ref, v_ref, bias_ref, lam_ref, nw_ref, o_ref,
                 qz_ref, s0_ref, s1_ref, m_ref, acc_ref, *, block, out_scale):
    h = pl.program_id(1)
    tq = block
    n_lane_tiles = block // LANES
    nq = q_ref.shape[1] // tq
    far_bias = tab_ref[N_BUCKETS - 1, h] * LOG2E
    ones = jnp.ones((block, LANES), BF16)

    def key_rows(kj):
        return pl.ds(pl.multiple_of(kj * block, block), block)

    def scores(kj, s_ref):
        s_ref[...] = _dot_nt(qz_ref[...], k_ref[0, key_rows(kj), :])

    def update(kj, s_ref, bias_tile, bias_const):
        s = s_ref[...]
        if bias_tile is not None:
            s = s + jnp.concatenate([bias_tile, bias_tile], axis=0)
        mx = s[:, :LANES]
        for c in range(1, n_lane_tiles):
            mx = jnp.maximum(mx, s[:, c * LANES:(c + 1) * LANES])
        m_prev = m_ref[...]
        m_new = jnp.maximum(m_prev, jnp.max(mx, axis=-1, keepdims=True) + bias_const)
        alpha = jnp.exp2(m_prev - m_new)
        shift = m_new - bias_const
        p = jnp.exp2(s - jnp.concatenate([shift] * n_lane_tiles, axis=1)).astype(BF16)
        v_ext = jnp.concatenate([v_ref[0, key_rows(kj), :], ones], axis=1)
        pv = jnp.dot(p, v_ext, preferred_element_type=F32)
        acc_ref[...] = jnp.concatenate([alpha, alpha], axis=1) * acc_ref[...] + pv
        m_ref[...] = m_new

    def q_block(qi, carry):
        q12 = q_ref[0, pl.ds(pl.multiple_of(qi * tq, tq), tq), :]
        lane = lax.broadcasted_iota(jnp.int32, q12.shape, 1)
        zero = jnp.zeros_like(q12)
        qz_ref[...] = jnp.concatenate([jnp.where(lane < DIFF_DH, q12, zero),
                                       jnp.where(lane >= DIFF_DH, q12, zero)], axis=0)
        m_ref[...] = jnp.full_like(m_ref, MASK_VALUE)
        acc_ref[...] = jnp.zeros_like(acc_ref)
        n_far = jnp.maximum(qi - 1, 0)
        scores(qi, s0_ref)
        scores(jnp.maximum(qi - 1, 0), s1_ref)
        update(qi, s0_ref, bias_ref[0, 0], 0.0)

        @pl.when(qi >= 1)
        def _():
            scores(0, s0_ref)
            update(qi - 1, s1_ref, bias_ref[0, 1], 0.0)

        def far_pair(i, c):
            scores(2 * i + 1, s1_ref)
            update(2 * i, s0_ref, None, far_bias)
            scores(jnp.minimum(2 * i + 2, n_far - 1), s0_ref)
            update(2 * i + 1, s1_ref, None, far_bias)
            return c

        lax.fori_loop(0, n_far // 2, far_pair, 0)

        @pl.when(n_far % 2 == 1)
        def _():
            update(n_far - 1, s0_ref, None, far_bias)

        acc = acc_ref[...]
        o_all = acc[:, :DIFF_DV] / acc[:, DIFF_DV:]
        o = o_all[:tq] - lam_ref[...] * o_all[tq:]
        ms = jnp.mean(o * o, axis=-1, keepdims=True)
        y = o * lax.rsqrt(ms + RMS_EPS) * nw_ref[...] * out_scale
        o_ref[0, pl.ds(pl.multiple_of(qi * tq, tq), tq), :] = y.astype(o_ref.dtype)
        return carry

    lax.fori_loop(0, nq, q_block, 0)


def _attn(proj3, bias_tiles, rel_bias, lam, norm_w, *, out_scale):
    B, S, _ = proj3.shape
    H = DIFF_HEADS
    block = ATT_BLOCK
    n_near = bias_tiles.shape[1]
    assert n_near == 2, "kernel visits exactly one off-diagonal bias tile per query block"
    q_blk = 4 * HGRN_HEADS
    k_blk = q_blk + H
    v_blk = k_blk + H
    seq_spec = lambda j: pl.BlockSpec((1, S, LANES), lambda b, h, j=j: (b, 0, j + h))
    vmem = (4 * 2 * S * LANES * 2 + 2 * n_near * block * block * 4
            + 2 * block * LANES * 2 + 2 * 2 * block * block * 4 + 2 * block * 3 * LANES * 4
            + 6 * 2 * block * block * 4 + (8 << 20))
    return pl.pallas_call(
        functools.partial(_attn_kernel, block=block, out_scale=out_scale),
        grid=(B, H),
        in_specs=[
            pl.BlockSpec(memory_space=pltpu.SMEM),
            seq_spec(q_blk), seq_spec(k_blk), seq_spec(v_blk),
            pl.BlockSpec((1, n_near, block, block), lambda b, h: (h, 0, 0, 0)),
            pl.BlockSpec((1, LANES), lambda b, h: (0, 0)),
            pl.BlockSpec((1, DIFF_DV), lambda b, h: (0, 0)),
        ],
        out_specs=pl.BlockSpec((1, S, LANES), lambda b, h: (b, 0, h)),
        out_shape=jax.ShapeDtypeStruct((B, S, DIFF_WIDTH), BF16),
        scratch_shapes=[pltpu.VMEM((2 * block, LANES), BF16),
                        pltpu.VMEM((2 * block, block), F32),
                        pltpu.VMEM((2 * block, block), F32),
                        pltpu.VMEM((2 * block, LANES), F32),
                        pltpu.VMEM((2 * block, 2 * LANES), F32)],
        compiler_params=pltpu.CompilerParams(
            dimension_semantics=("arbitrary", "arbitrary"),
            vmem_limit_bytes=_vmem_limit(vmem)),
        name="attn",
    )(rel_bias, proj3, proj3, proj3, bias_tiles, lam, norm_w)


def _out_ffn_kernel(x_ref, oh_ref, od_ref, ada_ref, wo_ref, g1_ref, b1_ref,
                    wg_ref, wu_ref, wd_ref, g2_ref, b2_ref, o_ref, *, alpha):
    gate_m = ada_ref[0, 2:3, :]
    shift_f = ada_ref[0, 3:4, :]
    scale_f = ada_ref[0, 4:5, :]
    gate_f = ada_ref[0, 5:6, :]
    hw = oh_ref.shape[1]
    mix = (jnp.dot(oh_ref[...], wo_ref[:hw, :], preferred_element_type=F32)
           + jnp.dot(od_ref[...], wo_ref[hw:, :], preferred_element_type=F32))
    x1 = _layer_norm_rows(alpha * x_ref[...] + (1.0 + gate_m) * mix) * g1_ref[...] + b1_ref[...]
    u = (_layer_norm_rows(x1) * (1.0 + scale_f) + shift_f).astype(BF16)
    d_ff = wg_ref.shape[1]
    y = jnp.zeros(x1.shape, F32)
    for c0 in range(0, d_ff, FF_CHUNK):
        a = jnp.dot(u, wg_ref[:, c0:c0 + FF_CHUNK], preferred_element_type=F32)
        b = jnp.dot(u, wu_ref[:, c0:c0 + FF_CHUNK], preferred_element_type=F32)
        hid = (a * (1.0 / (1.0 + jnp.exp(-a))) * b).astype(BF16)
        y = y + jnp.dot(hid, wd_ref[c0:c0 + FF_CHUNK, :], preferred_element_type=F32)
    o_ref[...] = _layer_norm_rows(alpha * x1 + (1.0 + gate_f) * y) * g2_ref[...] + b2_ref[...]


def _out_ffn(x2, oh2, od2, ada3, w_out, ln1_g, ln1_b, w_gate, w_up, w_down, ln2_g, ln2_b,
             *, seq, alpha):
    N, D = x2.shape
    d_ff = w_gate.shape[1]
    tm = ROW_TILE
    per_batch = seq // tm
    row = lambda w: pl.BlockSpec((tm, w), lambda i: (i, 0))
    const = lambda shape: pl.BlockSpec(shape, lambda i: (0, 0), pipeline_mode=pl.Buffered(1))
    vec = pl.BlockSpec((1, D), lambda i: (0, 0))
    weights = (w_out.size + w_gate.size + w_up.size + w_down.size) * 2
    vmem = (weights + 2 * 2 * tm * D * 4 + 2 * 2 * tm * oh2.shape[1] * 2
            + 6 * tm * D * 4 + 4 * tm * FF_CHUNK * 4 + (6 << 20))
    return pl.pallas_call(
        functools.partial(_out_ffn_kernel, alpha=alpha),
        grid=(N // tm,),
        in_specs=[
            row(D), row(oh2.shape[1]), row(od2.shape[1]),
            pl.BlockSpec((1, ada3.shape[1], D), lambda i: (i // per_batch, 0, 0)),
            const(w_out.shape), vec, vec,
            const(w_gate.shape), const(w_up.shape), const(w_down.shape), vec, vec,
        ],
        out_specs=row(D),
        out_shape=jax.ShapeDtypeStruct((N, D), F32),
        compiler_params=pltpu.CompilerParams(
            dimension_semantics=("arbitrary",), vmem_limit_bytes=_vmem_limit(vmem)),
        name="out_ffn",
    )(x2, oh2, od2, ada3, w_out, ln1_g, ln1_b, w_gate, w_up, w_down, ln2_g, ln2_b)


def kernel(x, c, w_ada, b_ada, w_in, lb_logits, hgrn_norm_w, lam_q1, lam_k1, lam_q2, lam_k2,
           diff_norm_w, rel_bias, w_out, ln1_g, ln1_b, w_gate, w_up, w_down, ln2_g, ln2_b):
    B, S, D = x.shape
    depth = w_ada.shape[0]
    assert S % ROW_TILE == 0 and S % HGRN_TILE == 0 and S % ATT_BLOCK == 0
    assert HGRN_TILE % CHUNK == 0 and w_in.shape[2] % PROJ_CHUNK == 0
    assert w_gate.shape[2] % FF_CHUNK == 0
    alpha = (2.0 * depth) ** 0.25
    n_near = _num_near_tiles(ATT_BLOCK)
    bias_tiles = _bias_tiles(rel_bias, block=ATT_BLOCK, n_near=n_near)
    q_cols = (4 * HGRN_WIDTH, 4 * HGRN_WIDTH + DIFF_WIDTH)
    for l in range(depth):
        lam_init = 0.8 - 0.6 * math.exp(-0.3 * l)
        ada, lb, lam = _prep(c, w_ada[l], b_ada[l][None, :], lb_logits,
                             lam_q1[l][None, :], lam_k1[l][None, :],
                             lam_q2[l][None, :], lam_k2[l][None, :],
                             layer=l, lam_init=lam_init)
        ada3 = ada.reshape(B, 6, D)
        x2 = x.reshape(B * S, D)
        proj = _in_proj(x2, ada3, w_in[l].astype(BF16), seq=S,
                        q_cols=q_cols, q_scale=LOG2E * DIFF_DH ** -0.5)
        proj3 = proj.reshape(B, S, proj.shape[1])
        o_h = _hgrn(proj3, lb, hgrn_norm_w[l][None, :])
        o_d = _attn(proj3, bias_tiles, rel_bias, lam, diff_norm_w[l][None, :],
                    out_scale=1.0 - lam_init)
        out = _out_ffn(x2, o_h.reshape(B * S, -1), o_d.reshape(B * S, -1), ada3,
                       w_out[l].astype(BF16), ln1_g[l][None, :], ln1_b[l][None, :],
                       w_gate[l].astype(BF16), w_up[l].astype(BF16), w_down[l].astype(BF16),
                       ln2_g[l][None, :], ln2_b[l][None, :], seq=S, alpha=alpha)
        x = out.reshape(B, S, D)
    return x
```

```python
import functools
import math

import jax
import jax.numpy as jnp
import numpy as np
from jax import lax
from jax.experimental import pallas as pl
from jax.experimental.pallas import tpu as pltpu

HGRN_HEADS = 4
HGRN_DK = 128
HGRN_DV = 128
HGRN_WIDTH = HGRN_HEADS * HGRN_DV
DIFF_HEADS = 4
DIFF_DH = 64
DIFF_DV = 2 * DIFF_DH
DIFF_WIDTH = DIFF_HEADS * DIFF_DV
N_BUCKETS = 32
MAX_DISTANCE = 128
CHUNK = 64
LN_EPS = 1e-5
RMS_EPS = 1e-6
LOG2E = math.log2(math.e)

LANES = 128
V7X_VMEM_BYTES = 64 * 1024 * 1024

ROW_TILE = 512
HGRN_TILE = 512
HGRN_GROUP = 4
ATT_BLOCK = 512
FF_CHUNK = 256
PROJ_CHUNK = 512
MASK_VALUE = -1e30

BF16 = jnp.bfloat16
F32 = jnp.float32


def _vmem_limit(block_bytes):
    return int(min(block_bytes, V7X_VMEM_BYTES - 4 * 1024 * 1024))


def _layer_norm_rows(x):
    mu = jnp.mean(x, axis=-1, keepdims=True)
    xc = x - mu
    var = jnp.mean(xc * xc, axis=-1, keepdims=True)
    return xc * lax.rsqrt(var + LN_EPS)


def _sigmoid_pair(z):
    t = jnp.exp(-jnp.abs(z))
    r = 1.0 / (1.0 + t)
    tr = t * r
    pos = z >= 0
    return jnp.where(pos, r, tr), jnp.where(pos, tr, r)


def _dot_nt(a, b):
    return lax.dot_general(a, b, (((1,), (1,)), ((), ())), preferred_element_type=F32)


def _dot_tn(a, b):
    return lax.dot_general(a, b, (((0,), (0,)), ((), ())), preferred_element_type=F32)


def _prep_kernel(c_ref, w_ref, b_ref, lbl_ref, q1_ref, k1_ref, q2_ref, k2_ref,
                 ada_ref, lb_ref, lam_ref, *, layer, lam_init):
    c = c_ref[...]
    sc = c * (1.0 / (1.0 + jnp.exp(-c)))
    ada_ref[...] = jnp.dot(sc.astype(BF16), w_ref[...].astype(BF16),
                           preferred_element_type=F32) + b_ref[...]

    @pl.when(pl.program_id(0) == 0)
    def _():
        n_rows = lbl_ref.shape[0]
        rows = [lbl_ref[r:r + 1, :] for r in range(n_rows)]
        mx = functools.reduce(jnp.maximum, rows)
        es = [jnp.exp(r - mx) for r in rows]
        den = functools.reduce(lambda a, b: a + b, es)
        num = functools.reduce(lambda a, b: a + b, es[:layer + 1])
        lb_ref[...] = num / den
        s1 = jnp.sum(q1_ref[...] * k1_ref[...], axis=-1, keepdims=True)
        s2 = jnp.sum(q2_ref[...] * k2_ref[...], axis=-1, keepdims=True)
        lam = jnp.exp(s1) - jnp.exp(s2) + lam_init
        lam_ref[...] = jnp.broadcast_to(lam, lam_ref.shape)


def _prep(c, w_ada, b_ada, lb_logits, lam_q1, lam_k1, lam_q2, lam_k2, *, layer, lam_init):
    B, D = c.shape
    n_out = w_ada.shape[1]
    col = D
    width = lb_logits.shape[1]
    small = lambda shape: pl.BlockSpec(shape, lambda j: (0,) * len(shape))
    return pl.pallas_call(
        functools.partial(_prep_kernel, layer=layer, lam_init=lam_init),
        grid=(n_out // col,),
        in_specs=[
            small((B, D)),
            pl.BlockSpec((D, col), lambda j: (0, j)),
            pl.BlockSpec((1, col), lambda j: (0, j)),
            small(lb_logits.shape),
            small((1, DIFF_DH)), small((1, DIFF_DH)), small((1, DIFF_DH)), small((1, DIFF_DH)),
        ],
        out_specs=[
            pl.BlockSpec((B, col), lambda j: (0, j)),
            small((1, width)),
            small((1, LANES)),
        ],
        out_shape=[
            jax.ShapeDtypeStruct((B, n_out), F32),
            jax.ShapeDtypeStruct((1, width), F32),
            jax.ShapeDtypeStruct((1, LANES), F32),
        ],
        compiler_params=pltpu.CompilerParams(
            dimension_semantics=("arbitrary",),
            vmem_limit_bytes=_vmem_limit(4 * D * col * 4 + (8 << 20))),
        name="prep",
    )(c, w_ada, b_ada, lb_logits, lam_q1, lam_k1, lam_q2, lam_k2)


def _t5_bucket_np(dist):
    max_exact = N_BUCKETS // 2
    d = np.maximum(dist, 1).astype(np.float32)
    large = max_exact + (np.log(d / max_exact) / math.log(MAX_DISTANCE / max_exact)
                         * (N_BUCKETS - max_exact)).astype(np.int32)
    large = np.minimum(large, N_BUCKETS - 1)
    return np.where(dist < max_exact, dist, large)


def _num_near_tiles(block):
    first_last = int(np.argmax(_t5_bucket_np(np.arange(0, 4 * MAX_DISTANCE)) == N_BUCKETS - 1))
    assert np.all(_t5_bucket_np(np.arange(first_last, 64 * MAX_DISTANCE)) == N_BUCKETS - 1)
    t = 0
    while t * block - (block - 1) < first_last:
        t += 1
    return t


def _bias_kernel(tab_ref, o_ref, *, block):
    h = pl.program_id(0)
    t = pl.program_id(1)
    r = lax.broadcasted_iota(jnp.int32, (block, block), 0)
    c = lax.broadcasted_iota(jnp.int32, (block, block), 1)
    dist = r - c + t * block
    rel = jnp.maximum(dist, 0)
    max_exact = N_BUCKETS // 2
    d = jnp.maximum(rel, 1).astype(F32)
    large = max_exact + (jnp.log(d / max_exact) / math.log(MAX_DISTANCE / max_exact)
                         * (N_BUCKETS - max_exact)).astype(jnp.int32)
    large = jnp.minimum(large, N_BUCKETS - 1)
    bucket = jnp.where(rel < max_exact, rel, large)
    bias = jnp.zeros((block, block), F32)
    for b in range(N_BUCKETS):
        bias = jnp.where(bucket == b, tab_ref[b, h], bias)
    o_ref[0, 0] = jnp.where(dist >= 0, bias * LOG2E, MASK_VALUE)


def _bias_tiles(rel_bias, *, block, n_near):
    H = rel_bias.shape[1]
    return pl.pallas_call(
        functools.partial(_bias_kernel, block=block),
        grid=(H, n_near),
        in_specs=[pl.BlockSpec(memory_space=pltpu.SMEM)],
        out_specs=pl.BlockSpec((1, 1, block, block), lambda h, t: (h, t, 0, 0)),
        out_shape=jax.ShapeDtypeStruct((H, n_near, block, block), F32),
        compiler_params=pltpu.CompilerParams(dimension_semantics=("arbitrary", "arbitrary")),
        name="bias_tiles",
    )(rel_bias)


def _in_proj_kernel(x_ref, ada_ref, w_ref, o_ref, *, col_scales):
    x = x_ref[...]
    shift = ada_ref[0, 0:1, :]
    scale = ada_ref[0, 1:2, :]
    u = (_layer_norm_rows(x) * (1.0 + scale) + shift).astype(BF16)
    n_out = o_ref.shape[1]
    for c0 in range(0, n_out, PROJ_CHUNK):
        acc = jnp.dot(u, w_ref[:, c0:c0 + PROJ_CHUNK], preferred_element_type=F32)
        for lo, hi, s in col_scales:
            if lo <= c0 < hi:
                acc = acc * s
        o_ref[:, c0:c0 + PROJ_CHUNK] = acc.astype(o_ref.dtype)


def _in_proj(x2, ada3, w_in, *, seq, col_scales):
    N, D = x2.shape
    n_out = w_in.shape[1]
    tm = ROW_TILE
    per_batch = seq // tm
    assert all(lo % PROJ_CHUNK == 0 and hi % PROJ_CHUNK == 0 for lo, hi, _ in col_scales)
    vmem = (2 * tm * D * 4 + D * n_out * 2 + 2 * tm * n_out * 2 + 2 * 8 * D * 4
            + 3 * tm * D * 4 + 2 * tm * PROJ_CHUNK * 4 + (4 << 20))
    return pl.pallas_call(
        functools.partial(_in_proj_kernel, col_scales=col_scales),
        grid=(N // tm,),
        in_specs=[
            pl.BlockSpec((tm, D), lambda i: (i, 0)),
            pl.BlockSpec((1, ada3.shape[1], D), lambda i: (i // per_batch, 0, 0)),
            pl.BlockSpec((D, n_out), lambda i: (0, 0), pipeline_mode=pl.Buffered(1)),
        ],
        out_specs=pl.BlockSpec((tm, n_out), lambda i: (i, 0)),
        out_shape=jax.ShapeDtypeStruct((N, n_out), BF16),
        compiler_params=pltpu.CompilerParams(
            dimension_semantics=("arbitrary",), vmem_limit_bytes=_vmem_limit(vmem)),
        name="in_proj",
    )(x2, ada3, w_in)


def _decay_matrix():
    n = 2 * CHUNK + 16
    r = lax.broadcasted_iota(jnp.int32, (n, CHUNK), 0)
    s = lax.broadcasted_iota(jnp.int32, (n, CHUNK), 1)
    upto_mid = (s < CHUNK // 2).astype(F32)
    d1 = (s <= r).astype(F32) - upto_mid
    d2 = (s > r - CHUNK).astype(F32)
    return jnp.where(r < CHUNK, d1, jnp.where(r < 2 * CHUNK, d2,
                     jnp.where(r < 2 * CHUNK + 8, upto_mid, 1.0))).astype(BF16)


def _split3(x):
    hi = x.astype(BF16)
    r1 = x - hi.astype(F32)
    mid = r1.astype(BF16)
    lo = (r1 - mid.astype(F32)).astype(BF16)
    return hi, mid, lo


def _hgrn_kernel(q_ref, f_ref, i_ref, g_ref, lb_ref, nw_ref, o_ref,
                 st_ref, qd_ref, kd_ref, qin_ref, kst_ref, dl_ref, of_ref):
    @pl.when(pl.program_id(1) == 0)
    def _():
        st_ref[...] = jnp.zeros_like(st_ref)

    lb = lb_ref[...]
    one_m_lb = 1.0 - lb
    nw = nw_ref[...]
    n_chunks = q_ref.shape[1] // CHUNK
    tri_r = lax.broadcasted_iota(jnp.int32, (CHUNK, CHUNK), 0)
    tri_c = lax.broadcasted_iota(jnp.int32, (CHUNK, CHUNK), 1)
    causal = tri_r >= tri_c
    decay = _decay_matrix()
    decay3 = jnp.concatenate([decay, decay, decay], axis=1)
    sub = CHUNK // 8

    half_span = 0.5 * one_m_lb
    f_mid = lb + half_span

    def chunk_rows(n):
        return pl.ds(pl.multiple_of(n * CHUNK, CHUNK), CHUNK)

    def heads():
        return [slice(h * HGRN_DK, (h + 1) * HGRN_DK) for h in range(HGRN_HEADS)]


    def gates(n):
        x = f_ref[0, chunk_rows(n), :].astype(F32)
        th = half_span * jnp.tanh(x)
        lf2 = jnp.log2(f_mid + th)
        return half_span - th, jnp.concatenate(_split3(lf2), axis=0)

    def operands(n, kc, rel):
        rows = chunk_rows(n)
        q = q_ref[0, rows, :].astype(F32)
        d_mid = rel[:CHUNK]
        e1 = jnp.exp2(d_mid)
        qd_f = q * e1
        qd_ref[rows, :] = qd_f.astype(BF16)
        kd_ref[rows, :] = (kc * jnp.exp2(-d_mid)).astype(BF16)
        e_mid = jnp.exp2(rel[2 * CHUNK:2 * CHUNK + 8])
        qin_ref[rows, :] = (qd_f.reshape(sub, 8, -1) * e_mid[None]).reshape(qd_f.shape).astype(BF16)
        kst_ref[rows, :] = (kc * jnp.exp2(rel[CHUNK:2 * CHUNK])).astype(BF16)
        dl_ref[pl.ds(pl.multiple_of(n * 8, 8), 8), :] = jnp.exp2(rel[2 * CHUNK + 8:])

    def prep_group(i, carry):
        ns = [i * HGRN_GROUP + j for j in range(HGRN_GROUP)]
        g1 = [gates(n) for n in ns]
        rels = [jnp.dot(decay3, parts, preferred_element_type=F32) for _, parts in g1]
        for n, (kc, _), rel in zip(ns, g1, rels):
            operands(n, kc, rel)
        return carry

    lax.fori_loop(0, n_chunks // HGRN_GROUP, prep_group, 0)

    def mix_group(i, carry):
        ns = [i * HGRN_GROUP + j for j in range(HGRN_GROUP)]
        local = []
        for n in ns:
            rows = chunk_rows(n)
            v = i_ref[0, rows, :]
            per_head = []
            for sl in heads():
                a = jnp.where(causal, _dot_nt(qd_ref[rows, sl], kd_ref[rows, sl]), 0.0)
                per_head.append((a.astype(BF16), _dot_tn(v[:, sl], kst_ref[rows, sl])))
            local.append((rows, v, per_head))
        for n, (rows, v, per_head) in zip(ns, local):
            d_last = dl_ref[pl.ds(pl.multiple_of(n * 8, 8), 8), :]
            for h, (sl, (a, upd)) in enumerate(zip(heads(), per_head)):
                st = st_ref[h]
                of_ref[rows, sl] = (jnp.dot(a, v[:, sl], preferred_element_type=F32)
                                    + _dot_nt(qin_ref[rows, sl], st.astype(BF16)))
                st_new = st.reshape(HGRN_DV // 8, 8, HGRN_DK) * d_last[None, :, sl]
                st_ref[h] = st_new.reshape(HGRN_DV, HGRN_DK) + upd
        return carry

    lax.fori_loop(0, n_chunks // HGRN_GROUP, mix_group, 0)

    def norm_block(n, carry):
        rows = chunk_rows(n)
        hg = g_ref[0, rows, :].astype(F32)
        gate = hg + hg * jnp.tanh(hg)
        for sl in heads():
            o = of_ref[rows, sl]
            ms = jnp.mean(o * o, axis=-1, keepdims=True)
            o_ref[0, rows, sl] = (o * lax.rsqrt(ms + RMS_EPS) * nw * gate[:, sl]).astype(o_ref.dtype)
        return carry

    lax.fori_loop(0, n_chunks, norm_block, 0, unroll=2)


def _hgrn(proj3, lb, norm_w):
    B, S, _ = proj3.shape
    ts = HGRN_TILE
    W = HGRN_WIDTH
    spec = lambda j: pl.BlockSpec((1, ts, W), lambda b, s, j=j: (b, s, j))
    vmem = 5 * 2 * ts * W * 2 + HGRN_HEADS * HGRN_DK * HGRN_DV * 4 + (16 << 20)
    return pl.pallas_call(
        _hgrn_kernel,
        grid=(B, S // ts),
        in_specs=[spec(0), spec(1), spec(2), spec(3),
                  pl.BlockSpec((1, W), lambda b, s: (0, 0)),
                  pl.BlockSpec((1, HGRN_DV), lambda b, s: (0, 0))],
        out_specs=pl.BlockSpec((1, ts, W), lambda b, s: (b, s, 0)),
        out_shape=jax.ShapeDtypeStruct((B, S, W), BF16),
        scratch_shapes=[pltpu.VMEM((HGRN_HEADS, HGRN_DV, HGRN_DK), F32),
                        pltpu.VMEM((ts, W), BF16), pltpu.VMEM((ts, W), BF16),
                        pltpu.VMEM((ts, W), BF16), pltpu.VMEM((ts, W), BF16),
                        pltpu.VMEM((ts // CHUNK * 8, W), F32),
                        pltpu.VMEM((ts, W), F32)],
        compiler_params=pltpu.CompilerParams(
            dimension_semantics=("arbitrary", "arbitrary"), vmem_limit_bytes=_vmem_limit(vmem)),
        name="hgrn",
    )(proj3, proj3, proj3, proj3, lb, norm_w)


def _attn_kernel(tab_ref, q_ref, k_ref, v_ref, bias_ref, lam_ref, nw_ref, o_ref,
                 qz_ref, s0_ref, s1_ref, m_ref, acc_ref, *, block, out_scale):
    h = pl.program_id(1)
    tq = block
    n_lane_tiles = block // LANES
    nq = q_ref.shape[1] // tq
    far_bias = tab_ref[N_BUCKETS - 1, h] * LOG2E
    ones = jnp.ones((block, LANES), BF16)

    def key_rows(kj):
        return pl.ds(pl.multiple_of(kj * block, block), block)

    def scores(kj, s_ref):
        s_ref[...] = _dot_nt(qz_ref[...], k_ref[0, key_rows(kj), :])

    def update(kj, s_ref, bias_tile, bias_const):
        s = s_ref[...]
        if bias_tile is not None:
            s = s + jnp.concatenate([bias_tile, bias_tile], axis=0)
        mx = s[:, :LANES]
        for c in range(1, n_lane_tiles):
            mx = jnp.maximum(mx, s[:, c * LANES:(c + 1) * LANES])
        m_prev = m_ref[...]
        m_new = jnp.maximum(m_prev, jnp.max(mx, axis=-1, keepdims=True) + bias_const)
        alpha = jnp.exp2(m_prev - m_new)
        shift = m_new - bias_const
        p = jnp.exp2(s - jnp.concatenate([shift] * n_lane_tiles, axis=1)).astype(BF16)
        v_ext = jnp.concatenate([v_ref[0, key_rows(kj), :], ones], axis=1)
        pv = jnp.dot(p, v_ext, preferred_element_type=F32)
        acc_ref[...] = jnp.concatenate([alpha, alpha], axis=1) * acc_ref[...] + pv
        m_ref[...] = m_new

    def q_block(qi, carry):
        q12 = q_ref[0, pl.ds(pl.multiple_of(qi * tq, tq), tq), :]
        lane = lax.broadcasted_iota(jnp.int32, q12.shape, 1)
        zero = jnp.zeros_like(q12)
        qz_ref[...] = jnp.concatenate([jnp.where(lane < DIFF_DH, q12, zero),
                                       jnp.where(lane >= DIFF_DH, q12, zero)], axis=0)
        m_ref[...] = jnp.full_like(m_ref, MASK_VALUE)
        acc_ref[...] = jnp.zeros_like(acc_ref)
        n_far = jnp.maximum(qi - 1, 0)
        scores(qi, s0_ref)
        scores(jnp.maximum(qi - 1, 0), s1_ref)
        update(qi, s0_ref, bias_ref[0, 0], 0.0)

        @pl.when(qi >= 1)
        def _():
            scores(0, s0_ref)
            update(qi - 1, s1_ref, bias_ref[0, 1], 0.0)

        def far_pair(i, c):
            scores(2 * i + 1, s1_ref)
            update(2 * i, s0_ref, None, far_bias)
            scores(jnp.minimum(2 * i + 2, n_far - 1), s0_ref)
            update(2 * i + 1, s1_ref, None, far_bias)
            return c

        lax.fori_loop(0, n_far // 2, far_pair, 0)

        @pl.when(n_far % 2 == 1)
        def _():
            update(n_far - 1, s0_ref, None, far_bias)

        acc = acc_ref[...]
        o_all = acc[:, :DIFF_DV] / acc[:, DIFF_DV:]
        o = o_all[:tq] - lam_ref[...] * o_all[tq:]
        ms = jnp.mean(o * o, axis=-1, keepdims=True)
        y = o * lax.rsqrt(ms + RMS_EPS) * nw_ref[...] * out_scale
        o_ref[0, pl.ds(pl.multiple_of(qi * tq, tq), tq), :] = y.astype(o_ref.dtype)
        return carry

    lax.fori_loop(0, nq, q_block, 0)


def _attn(proj3, bias_tiles, rel_bias, lam, norm_w, *, out_scale):
    B, S, _ = proj3.shape
    H = DIFF_HEADS
    block = ATT_BLOCK
    n_near = bias_tiles.shape[1]
    assert n_near == 2, "kernel visits exactly one off-diagonal bias tile per query block"
    q_blk = 4 * HGRN_HEADS
    k_blk = q_blk + H
    v_blk = k_blk + H
    seq_spec = lambda j: pl.BlockSpec((1, S, LANES), lambda b, h, j=j: (b, 0, j + h))
    vmem = (4 * 2 * S * LANES * 2 + 2 * n_near * block * block * 4
            + 2 * block * LANES * 2 + 2 * 2 * block * block * 4 + 2 * block * 3 * LANES * 4
            + 6 * 2 * block * block * 4 + (8 << 20))
    return pl.pallas_call(
        functools.partial(_attn_kernel, block=block, out_scale=out_scale),
        grid=(B, H),
        in_specs=[
            pl.BlockSpec(memory_space=pltpu.SMEM),
            seq_spec(q_blk), seq_spec(k_blk), seq_spec(v_blk),
            pl.BlockSpec((1, n_near, block, block), lambda b, h: (h, 0, 0, 0)),
            pl.BlockSpec((1, LANES), lambda b, h: (0, 0)),
            pl.BlockSpec((1, DIFF_DV), lambda b, h: (0, 0)),
        ],
        out_specs=pl.BlockSpec((1, S, LANES), lambda b, h: (b, 0, h)),
        out_shape=jax.ShapeDtypeStruct((B, S, DIFF_WIDTH), BF16),
        scratch_shapes=[pltpu.VMEM((2 * block, LANES), BF16),
                        pltpu.VMEM((2 * block, block), F32),
                        pltpu.VMEM((2 * block, block), F32),
                        pltpu.VMEM((2 * block, LANES), F32),
                        pltpu.VMEM((2 * block, 2 * LANES), F32)],
        compiler_params=pltpu.CompilerParams(
            dimension_semantics=("arbitrary", "arbitrary"),
            vmem_limit_bytes=_vmem_limit(vmem)),
        name="attn",
    )(rel_bias, proj3, proj3, proj3, bias_tiles, lam, norm_w)


def _out_ffn_kernel(x_ref, oh_ref, od_ref, ada_ref, wo_ref, g1_ref, b1_ref,
                    wg_ref, wu_ref, wd_ref, g2_ref, b2_ref, o_ref, *, alpha):
    gate_m = ada_ref[0, 2:3, :]
    shift_f = ada_ref[0, 3:4, :]
    scale_f = ada_ref[0, 4:5, :]
    gate_f = ada_ref[0, 5:6, :]
    hw = oh_ref.shape[1]
    mix = (jnp.dot(oh_ref[...], wo_ref[:hw, :], preferred_element_type=F32)
           + jnp.dot(od_ref[...], wo_ref[hw:, :], preferred_element_type=F32))
    x1 = _layer_norm_rows(alpha * x_ref[...] + (1.0 + gate_m) * mix) * g1_ref[...] + b1_ref[...]
    u = (_layer_norm_rows(x1) * (1.0 + scale_f) + shift_f).astype(BF16)
    d_ff = wg_ref.shape[1]
    y = jnp.zeros(x1.shape, F32)
    for c0 in range(0, d_ff, FF_CHUNK):
        a = jnp.dot(u, wg_ref[:, c0:c0 + FF_CHUNK], preferred_element_type=F32)
        b = jnp.dot(u, wu_ref[:, c0:c0 + FF_CHUNK], preferred_element_type=F32)
        hid = (a * (1.0 / (1.0 + jnp.exp(-a))) * b).astype(BF16)
        y = y + jnp.dot(hid, wd_ref[c0:c0 + FF_CHUNK, :], preferred_element_type=F32)
    o_ref[...] = _layer_norm_rows(alpha * x1 + (1.0 + gate_f) * y) * g2_ref[...] + b2_ref[...]


def _out_ffn(x2, oh2, od2, ada3, w_out, ln1_g, ln1_b, w_gate, w_up, w_down, ln2_g, ln2_b,
             *, seq, alpha):
    N, D = x2.shape
    d_ff = w_gate.shape[1]
    tm = ROW_TILE
    per_batch = seq // tm
    row = lambda w: pl.BlockSpec((tm, w), lambda i: (i, 0))
    const = lambda shape: pl.BlockSpec(shape, lambda i: (0, 0), pipeline_mode=pl.Buffered(1))
    vec = pl.BlockSpec((1, D), lambda i: (0, 0))
    weights = (w_out.size + w_gate.size + w_up.size + w_down.size) * 2
    vmem = (weights + 2 * 2 * tm * D * 4 + 2 * 2 * tm * oh2.shape[1] * 2
            + 6 * tm * D * 4 + 4 * tm * FF_CHUNK * 4 + (6 << 20))
    return pl.pallas_call(
        functools.partial(_out_ffn_kernel, alpha=alpha),
        grid=(N // tm,),
        in_specs=[
            row(D), row(oh2.shape[1]), row(od2.shape[1]),
            pl.BlockSpec((1, ada3.shape[1], D), lambda i: (i // per_batch, 0, 0)),
            const(w_out.shape), vec, vec,
            const(w_gate.shape), const(w_up.shape), const(w_down.shape), vec, vec,
        ],
        out_specs=row(D),
        out_shape=jax.ShapeDtypeStruct((N, D), F32),
        compiler_params=pltpu.CompilerParams(
            dimension_semantics=("arbitrary",), vmem_limit_bytes=_vmem_limit(vmem)),
        name="out_ffn",
    )(x2, oh2, od2, ada3, w_out, ln1_g, ln1_b, w_gate, w_up, w_down, ln2_g, ln2_b)


def kernel(x, c, w_ada, b_ada, w_in, lb_logits, hgrn_norm_w, lam_q1, lam_k1, lam_q2, lam_k2,
           diff_norm_w, rel_bias, w_out, ln1_g, ln1_b, w_gate, w_up, w_down, ln2_g, ln2_b):
    B, S, D = x.shape
    depth = w_ada.shape[0]
    assert S % ROW_TILE == 0 and S % HGRN_TILE == 0 and S % ATT_BLOCK == 0
    assert HGRN_TILE % CHUNK == 0 and w_in.shape[2] % PROJ_CHUNK == 0
    assert w_gate.shape[2] % FF_CHUNK == 0
    alpha = (2.0 * depth) ** 0.25
    n_near = _num_near_tiles(ATT_BLOCK)
    bias_tiles = _bias_tiles(rel_bias, block=ATT_BLOCK, n_near=n_near)
    col_scales = ((0, HGRN_WIDTH, HGRN_DK ** -0.5),
                  (HGRN_WIDTH, 2 * HGRN_WIDTH, 0.5),
                  (3 * HGRN_WIDTH, 4 * HGRN_WIDTH, 0.5),
                  (4 * HGRN_WIDTH, 4 * HGRN_WIDTH + DIFF_WIDTH, LOG2E * DIFF_DH ** -0.5))
    for l in range(depth):
        lam_init = 0.8 - 0.6 * math.exp(-0.3 * l)
        ada, lb, lam = _prep(c, w_ada[l], b_ada[l][None, :], lb_logits,
                             lam_q1[l][None, :], lam_k1[l][None, :],
                             lam_q2[l][None, :], lam_k2[l][None, :],
                             layer=l, lam_init=lam_init)
        ada3 = ada.reshape(B, 6, D)
        x2 = x.reshape(B * S, D)
        proj = _in_proj(x2, ada3, w_in[l].astype(BF16), seq=S, col_scales=col_scales)
        proj3 = proj.reshape(B, S, proj.shape[1])
        o_h = _hgrn(proj3, lb, hgrn_norm_w[l][None, :])
        o_d = _attn(proj3, bias_tiles, rel_bias, lam, diff_norm_w[l][None, :],
                    out_scale=1.0 - lam_init)
        out = _out_ffn(x2, o_h.reshape(B * S, -1), o_d.reshape(B * S, -1), ada3,
                       w_out[l].astype(BF16), ln1_g[l][None, :], ln1_b[l][None, :],
                       w_gate[l].astype(BF16), w_up[l].astype(BF16), w_down[l].astype(BF16),
                       ln2_g[l][None, :], ln2_b[l][None, :], seq=S, alpha=alpha)
        x = out.reshape(B, S, D)
    return x
```

```python
import functools
import math

import jax
import jax.numpy as jnp
import numpy as np
from jax import lax
from jax.experimental import pallas as pl
from jax.experimental.pallas import tpu as pltpu

HGRN_HEADS = 4
HGRN_DK = 128
HGRN_DV = 128
HGRN_WIDTH = HGRN_HEADS * HGRN_DV
DIFF_HEADS = 4
DIFF_DH = 64
DIFF_DV = 2 * DIFF_DH
DIFF_WIDTH = DIFF_HEADS * DIFF_DV
N_BUCKETS = 32
MAX_DISTANCE = 128
CHUNK = 64
LN_EPS = 1e-5
RMS_EPS = 1e-6
LOG2E = math.log2(math.e)

LANES = 128
V7X_VMEM_BYTES = 64 * 1024 * 1024

ROW_TILE = 512
HGRN_TILE = 512
HGRN_GROUP = 4
ATT_BLOCK = 512
FF_CHUNK = 256
PROJ_CHUNK = 512
MASK_VALUE = -1e30

BF16 = jnp.bfloat16
F32 = jnp.float32


def _vmem_limit(block_bytes):
    return int(min(block_bytes, V7X_VMEM_BYTES - 4 * 1024 * 1024))


def _layer_norm_rows(x):
    mu = jnp.mean(x, axis=-1, keepdims=True)
    xc = x - mu
    var = jnp.mean(xc * xc, axis=-1, keepdims=True)
    return xc * lax.rsqrt(var + LN_EPS)


def _dot_nt(a, b):
    return lax.dot_general(a, b, (((1,), (1,)), ((), ())), preferred_element_type=F32)


def _dot_tn(a, b):
    return lax.dot_general(a, b, (((0,), (0,)), ((), ())), preferred_element_type=F32)


def _prep_kernel(c_ref, w_ref, b_ref, lbl_ref, q1_ref, k1_ref, q2_ref, k2_ref,
                 ada_ref, lb_ref, lam_ref, *, layer, lam_init):
    c = c_ref[...]
    sc = c * (1.0 / (1.0 + jnp.exp(-c)))
    ada_ref[...] = jnp.dot(sc.astype(BF16), w_ref[...].astype(BF16),
                           preferred_element_type=F32) + b_ref[...]

    @pl.when(pl.program_id(0) == 0)
    def _():
        n_rows = lbl_ref.shape[0]
        rows = [lbl_ref[r:r + 1, :] for r in range(n_rows)]
        mx = functools.reduce(jnp.maximum, rows)
        es = [jnp.exp(r - mx) for r in rows]
        den = functools.reduce(lambda a, b: a + b, es)
        num = functools.reduce(lambda a, b: a + b, es[:layer + 1])
        lb_ref[...] = num / den
        s1 = jnp.sum(q1_ref[...] * k1_ref[...], axis=-1, keepdims=True)
        s2 = jnp.sum(q2_ref[...] * k2_ref[...], axis=-1, keepdims=True)
        lam = jnp.exp(s1) - jnp.exp(s2) + lam_init
        lam_ref[...] = jnp.broadcast_to(lam, lam_ref.shape)


def _prep(c, w_ada, b_ada, lb_logits, lam_q1, lam_k1, lam_q2, lam_k2, *, layer, lam_init):
    B, D = c.shape
    n_out = w_ada.shape[1]
    col = D
    width = lb_logits.shape[1]
    small = lambda shape: pl.BlockSpec(shape, lambda j: (0,) * len(shape))
    return pl.pallas_call(
        functools.partial(_prep_kernel, layer=layer, lam_init=lam_init),
        grid=(n_out // col,),
        in_specs=[
            small((B, D)),
            pl.BlockSpec((D, col), lambda j: (0, j)),
            pl.BlockSpec((1, col), lambda j: (0, j)),
            small(lb_logits.shape),
            small((1, DIFF_DH)), small((1, DIFF_DH)), small((1, DIFF_DH)), small((1, DIFF_DH)),
        ],
        out_specs=[
            pl.BlockSpec((B, col), lambda j: (0, j)),
            small((1, width)),
            small((1, LANES)),
        ],
        out_shape=[
            jax.ShapeDtypeStruct((B, n_out), F32),
            jax.ShapeDtypeStruct((1, width), F32),
            jax.ShapeDtypeStruct((1, LANES), F32),
        ],
        compiler_params=pltpu.CompilerParams(
            dimension_semantics=("arbitrary",),
            vmem_limit_bytes=_vmem_limit(4 * D * col * 4 + (8 << 20))),
        name="prep",
    )(c, w_ada, b_ada, lb_logits, lam_q1, lam_k1, lam_q2, lam_k2)


def _t5_bucket_np(dist):
    max_exact = N_BUCKETS // 2
    d = np.maximum(dist, 1).astype(np.float32)
    large = max_exact + (np.log(d / max_exact) / math.log(MAX_DISTANCE / max_exact)
                         * (N_BUCKETS - max_exact)).astype(np.int32)
    large = np.minimum(large, N_BUCKETS - 1)
    return np.where(dist < max_exact, dist, large)


def _num_near_tiles(block):
    first_last = int(np.argmax(_t5_bucket_np(np.arange(0, 4 * MAX_DISTANCE)) == N_BUCKETS - 1))
    assert np.all(_t5_bucket_np(np.arange(first_last, 64 * MAX_DISTANCE)) == N_BUCKETS - 1)
    t = 0
    while t * block - (block - 1) < first_last:
        t += 1
    return t


def _bias_kernel(tab_ref, o_ref, *, block):
    h = pl.program_id(0)
    t = pl.program_id(1)
    r = lax.broadcasted_iota(jnp.int32, (block, block), 0)
    c = lax.broadcasted_iota(jnp.int32, (block, block), 1)
    dist = r - c + t * block
    rel = jnp.maximum(dist, 0)
    max_exact = N_BUCKETS // 2
    d = jnp.maximum(rel, 1).astype(F32)
    large = max_exact + (jnp.log(d / max_exact) / math.log(MAX_DISTANCE / max_exact)
                         * (N_BUCKETS - max_exact)).astype(jnp.int32)
    large = jnp.minimum(large, N_BUCKETS - 1)
    bucket = jnp.where(rel < max_exact, rel, large)
    bias = jnp.zeros((block, block), F32)
    for b in range(N_BUCKETS):
        bias = jnp.where(bucket == b, tab_ref[b, h], bias)
    shifted = (bias - tab_ref[N_BUCKETS - 1, h]) * LOG2E
    o_ref[0, 0] = jnp.where(dist >= 0, shifted, MASK_VALUE)


def _bias_tiles(rel_bias, *, block, n_tiles):
    H = rel_bias.shape[1]
    return pl.pallas_call(
        functools.partial(_bias_kernel, block=block),
        grid=(H, n_tiles),
        in_specs=[pl.BlockSpec(memory_space=pltpu.SMEM)],
        out_specs=pl.BlockSpec((1, 1, block, block), lambda h, t: (h, t, 0, 0)),
        out_shape=jax.ShapeDtypeStruct((H, n_tiles, block, block), F32),
        compiler_params=pltpu.CompilerParams(dimension_semantics=("arbitrary", "arbitrary")),
        name="bias_tiles",
    )(rel_bias)


def _in_proj_kernel(x_ref, ada_ref, w_ref, o_ref, *, col_scales):
    x = x_ref[...]
    shift = ada_ref[0, 0:1, :]
    scale = ada_ref[0, 1:2, :]
    u = (_layer_norm_rows(x) * (1.0 + scale) + shift).astype(BF16)
    n_out = o_ref.shape[1]
    for c0 in range(0, n_out, PROJ_CHUNK):
        acc = jnp.dot(u, w_ref[:, c0:c0 + PROJ_CHUNK], preferred_element_type=F32)
        for lo, hi, s in col_scales:
            if lo <= c0 < hi:
                acc = acc * s
        o_ref[:, c0:c0 + PROJ_CHUNK] = acc.astype(o_ref.dtype)


def _in_proj(x2, ada3, w_in, *, seq, col_scales):
    N, D = x2.shape
    n_out = w_in.shape[1]
    tm = ROW_TILE
    per_batch = seq // tm
    assert all(lo % PROJ_CHUNK == 0 and hi % PROJ_CHUNK == 0 for lo, hi, _ in col_scales)
    vmem = (2 * tm * D * 4 + D * n_out * 2 + 2 * tm * n_out * 2 + 2 * 8 * D * 4
            + 3 * tm * D * 4 + 2 * tm * PROJ_CHUNK * 4 + (4 << 20))
    return pl.pallas_call(
        functools.partial(_in_proj_kernel, col_scales=col_scales),
        grid=(N // tm,),
        in_specs=[
            pl.BlockSpec((tm, D), lambda i: (i, 0)),
            pl.BlockSpec((1, ada3.shape[1], D), lambda i: (i // per_batch, 0, 0)),
            pl.BlockSpec((D, n_out), lambda i: (0, 0), pipeline_mode=pl.Buffered(1)),
        ],
        out_specs=pl.BlockSpec((tm, n_out), lambda i: (i, 0)),
        out_shape=jax.ShapeDtypeStruct((N, n_out), BF16),
        compiler_params=pltpu.CompilerParams(
            dimension_semantics=("arbitrary",), vmem_limit_bytes=_vmem_limit(vmem)),
        name="in_proj",
    )(x2, ada3, w_in)


def _decay_matrix():
    n = 2 * CHUNK + 16
    r = lax.broadcasted_iota(jnp.int32, (n, CHUNK), 0)
    s = lax.broadcasted_iota(jnp.int32, (n, CHUNK), 1)
    upto_mid = (s < CHUNK // 2).astype(F32)
    d1 = (s <= r).astype(F32) - upto_mid
    d2 = (s > r - CHUNK).astype(F32)
    return jnp.where(r < CHUNK, d1, jnp.where(r < 2 * CHUNK, d2,
                     jnp.where(r < 2 * CHUNK + 8, upto_mid, 1.0))).astype(BF16)


def _split3(x):
    hi = x.astype(BF16)
    r1 = x - hi.astype(F32)
    mid = r1.astype(BF16)
    lo = (r1 - mid.astype(F32)).astype(BF16)
    return hi, mid, lo


def _hgrn_kernel(q_ref, f_ref, i_ref, g_ref, lb_ref, nw_ref, o_ref,
                 st_ref, qd_ref, kd_ref, qin_ref, kst_ref, dl_ref, of_ref):
    @pl.when(pl.program_id(1) == 0)
    def _():
        st_ref[...] = jnp.zeros_like(st_ref)

    lb = lb_ref[...]
    one_m_lb = 1.0 - lb
    nw = nw_ref[...]
    n_chunks = q_ref.shape[1] // CHUNK
    tri_r = lax.broadcasted_iota(jnp.int32, (CHUNK, CHUNK), 0)
    tri_c = lax.broadcasted_iota(jnp.int32, (CHUNK, CHUNK), 1)
    causal = tri_r >= tri_c
    decay = _decay_matrix()
    decay3 = jnp.concatenate([decay, decay, decay], axis=1)
    sub = CHUNK // 8

    half_span = 0.5 * one_m_lb
    f_mid = lb + half_span

    def chunk_rows(n):
        return pl.ds(pl.multiple_of(n * CHUNK, CHUNK), CHUNK)

    def heads():
        return [slice(h * HGRN_DK, (h + 1) * HGRN_DK) for h in range(HGRN_HEADS)]


    def gates(n):
        x = f_ref[0, chunk_rows(n), :].astype(F32)
        th = half_span * jnp.tanh(x)
        lf2 = jnp.log2(f_mid + th)
        return half_span - th, jnp.concatenate(_split3(lf2), axis=0)

    def operands(n, kc, rel):
        rows = chunk_rows(n)
        q = q_ref[0, rows, :].astype(F32)
        d_mid = rel[:CHUNK]
        e1 = jnp.exp2(d_mid)
        qd_f = q * e1
        qd_ref[rows, :] = qd_f.astype(BF16)
        kd_ref[rows, :] = (kc * jnp.exp2(-d_mid)).astype(BF16)
        e_mid = jnp.exp2(rel[2 * CHUNK:2 * CHUNK + 8])
        qin_ref[rows, :] = (qd_f.reshape(sub, 8, -1) * e_mid[None]).reshape(qd_f.shape).astype(BF16)
        kst_ref[rows, :] = (kc * jnp.exp2(rel[CHUNK:2 * CHUNK])).astype(BF16)
        dl_ref[pl.ds(pl.multiple_of(n * 8, 8), 8), :] = jnp.exp2(rel[2 * CHUNK + 8:])

    def prep_group(i, carry):
        ns = [i * HGRN_GROUP + j for j in range(HGRN_GROUP)]
        g1 = [gates(n) for n in ns]
        rels = [jnp.dot(decay3, parts, preferred_element_type=F32) for _, parts in g1]
        for n, (kc, _), rel in zip(ns, g1, rels):
            operands(n, kc, rel)
        return carry

    lax.fori_loop(0, n_chunks // HGRN_GROUP, prep_group, 0)

    def mix_group(i, carry):
        ns = [i * HGRN_GROUP + j for j in range(HGRN_GROUP)]
        local = []
        for n in ns:
            rows = chunk_rows(n)
            v = i_ref[0, rows, :]
            per_head = []
            for sl in heads():
                a = jnp.where(causal, _dot_nt(qd_ref[rows, sl], kd_ref[rows, sl]), 0.0)
                per_head.append((a.astype(BF16), _dot_tn(v[:, sl], kst_ref[rows, sl])))
            local.append((rows, v, per_head))
        for n, (rows, v, per_head) in zip(ns, local):
            d_last = dl_ref[pl.ds(pl.multiple_of(n * 8, 8), 8), :]
            for h, (sl, (a, upd)) in enumerate(zip(heads(), per_head)):
                st = st_ref[h]
                of_ref[rows, sl] = (jnp.dot(a, v[:, sl], preferred_element_type=F32)
                                    + _dot_nt(qin_ref[rows, sl], st.astype(BF16)))
                st_new = st.reshape(HGRN_DV // 8, 8, HGRN_DK) * d_last[None, :, sl]
                st_ref[h] = st_new.reshape(HGRN_DV, HGRN_DK) + upd
        return carry

    lax.fori_loop(0, n_chunks // HGRN_GROUP, mix_group, 0)

    def norm_block(n, carry):
        rows = chunk_rows(n)
        hg = g_ref[0, rows, :].astype(F32)
        gate = hg + hg * jnp.tanh(hg)
        for sl in heads():
            o = of_ref[rows, sl]
            ms = jnp.mean(o * o, axis=-1, keepdims=True)
            o_ref[0, rows, sl] = (o * lax.rsqrt(ms + RMS_EPS) * nw * gate[:, sl]).astype(o_ref.dtype)
        return carry

    lax.fori_loop(0, n_chunks, norm_block, 0, unroll=2)


def _hgrn(proj3, lb, norm_w):
    B, S, _ = proj3.shape
    ts = HGRN_TILE
    W = HGRN_WIDTH
    spec = lambda j: pl.BlockSpec((1, ts, W), lambda b, s, j=j: (b, s, j))
    vmem = 5 * 2 * ts * W * 2 + HGRN_HEADS * HGRN_DK * HGRN_DV * 4 + (16 << 20)
    return pl.pallas_call(
        _hgrn_kernel,
        grid=(B, S // ts),
        in_specs=[spec(0), spec(1), spec(2), spec(3),
                  pl.BlockSpec((1, W), lambda b, s: (0, 0)),
                  pl.BlockSpec((1, HGRN_DV), lambda b, s: (0, 0))],
        out_specs=pl.BlockSpec((1, ts, W), lambda b, s: (b, s, 0)),
        out_shape=jax.ShapeDtypeStruct((B, S, W), BF16),
        scratch_shapes=[pltpu.VMEM((HGRN_HEADS, HGRN_DV, HGRN_DK), F32),
                        pltpu.VMEM((ts, W), BF16), pltpu.VMEM((ts, W), BF16),
                        pltpu.VMEM((ts, W), BF16), pltpu.VMEM((ts, W), BF16),
                        pltpu.VMEM((ts // CHUNK * 8, W), F32),
                        pltpu.VMEM((ts, W), F32)],
        compiler_params=pltpu.CompilerParams(
            dimension_semantics=("arbitrary", "arbitrary"), vmem_limit_bytes=_vmem_limit(vmem)),
        name="hgrn",
    )(proj3, proj3, proj3, proj3, lb, norm_w)


def _attn_schedule(nq):
    items = [(qi, kj) for qi in range(nq) for kj in range(qi + 1)]
    warm, tail = (0, 1), (nq - 1, 0)
    fresh = [warm] + items + [tail]
    prev = [warm, warm] + items
    return np.array([[f[0] for f in fresh], [f[1] for f in fresh],
                     [p[0] for p in prev], [p[1] for p in prev]], np.int32)


def _attn_kernel(sched_ref, q_ref, k_ref, v_ref, bias_ref, lam_ref, nw_ref, o_ref,
                 s0_ref, s1_ref, m_ref, acc_ref, *, block, out_scale):
    tq = block
    n_lane_tiles = block // LANES
    n_steps = sched_ref.shape[1]
    last_tile = bias_ref.shape[1] - 1
    ones = jnp.ones((block, LANES), BF16)

    def blk(j):
        return pl.ds(pl.multiple_of(j * block, block), block)

    def scores(step, s_ref):
        q12 = q_ref[0, blk(sched_ref[0, step]), :]
        lane = lax.broadcasted_iota(jnp.int32, q12.shape, 1)
        zero = jnp.zeros_like(q12)
        qz = jnp.concatenate([jnp.where(lane < DIFF_DH, q12, zero),
                              jnp.where(lane >= DIFF_DH, q12, zero)], axis=0)
        s_ref[...] = _dot_nt(qz, k_ref[0, blk(sched_ref[1, step]), :])

    def update(step, s_ref):
        qi = sched_ref[2, step]
        kj = sched_ref[3, step]
        bias = bias_ref[0, jnp.clip(qi - kj, 0, last_tile)]
        v_ext = jnp.concatenate([v_ref[0, blk(kj), :], ones], axis=1)
        s = s_ref[...] + jnp.concatenate([bias, bias], axis=0)
        mx = s[:, :LANES]
        for c in range(1, n_lane_tiles):
            mx = jnp.maximum(mx, s[:, c * LANES:(c + 1) * LANES])
        m_prev = jnp.where(kj == 0, MASK_VALUE, m_ref[...])
        m_new = jnp.maximum(m_prev, jnp.max(mx, axis=-1, keepdims=True))
        alpha = jnp.exp2(m_prev - m_new)
        p = jnp.exp2(s - jnp.concatenate([m_new] * n_lane_tiles, axis=1)).astype(BF16)
        pv = jnp.dot(p, v_ext, preferred_element_type=F32)
        acc_ref[qi] = jnp.concatenate([alpha, alpha], axis=1) * acc_ref[qi] + pv
        m_ref[...] = m_new

    @pl.when((pl.program_id(0) == 0) & (pl.program_id(1) == 0))
    def _():
        s1_ref[...] = jnp.zeros_like(s1_ref)
        m_ref[...] = jnp.full_like(m_ref, MASK_VALUE)
        acc_ref[...] = jnp.zeros_like(acc_ref)

    def step_pair(i, carry):
        scores(2 * i, s0_ref)
        update(2 * i, s1_ref)
        scores(2 * i + 1, s1_ref)
        update(2 * i + 1, s0_ref)
        return carry

    lax.fori_loop(0, n_steps // 2, step_pair, 0)

    def finalize(qi, carry):
        acc = acc_ref[qi]
        acc_ref[qi] = jnp.zeros_like(acc)
        o_all = acc[:, :DIFF_DV] / acc[:, DIFF_DV:]
        o = o_all[:tq] - lam_ref[...] * o_all[tq:]
        ms = jnp.mean(o * o, axis=-1, keepdims=True)
        y = o * lax.rsqrt(ms + RMS_EPS) * nw_ref[...] * out_scale
        o_ref[0, blk(qi), :] = y.astype(o_ref.dtype)
        return carry

    lax.fori_loop(0, acc_ref.shape[0], finalize, 0)


def _attn(proj3, bias_tiles, lam, norm_w, *, out_scale):
    B, S, _ = proj3.shape
    H = DIFF_HEADS
    block = ATT_BLOCK
    n_tiles = bias_tiles.shape[1]
    nq = S // block
    assert nq >= 2 and (nq * (nq + 1) // 2) % 2 == 0, "pipeline steps are unrolled in pairs"
    q_blk = 4 * HGRN_HEADS
    k_blk = q_blk + H
    v_blk = k_blk + H
    seq_spec = lambda j: pl.BlockSpec((1, S, LANES), lambda b, h, j=j: (b, 0, j + h))
    rows = 2 * block
    scratch = [pltpu.VMEM((rows, block), F32), pltpu.VMEM((rows, block), F32),
               pltpu.VMEM((rows, LANES), F32),
               pltpu.VMEM((nq, rows, 2 * LANES), F32)]
    scratch_bytes = rows * (2 * block * 4 + LANES * 4 + nq * 2 * LANES * 4)
    vmem = (4 * 2 * S * LANES * 2 + 2 * n_tiles * block * block * 4 + scratch_bytes
            + 5 * rows * block * 4 + (6 << 20))
    return pl.pallas_call(
        functools.partial(_attn_kernel, block=block, out_scale=out_scale),
        grid=(B, H),
        in_specs=[
            pl.BlockSpec(memory_space=pltpu.SMEM),
            seq_spec(q_blk), seq_spec(k_blk), seq_spec(v_blk),
            pl.BlockSpec((1, n_tiles, block, block), lambda b, h: (h, 0, 0, 0)),
            pl.BlockSpec((1, LANES), lambda b, h: (0, 0)),
            pl.BlockSpec((1, DIFF_DV), lambda b, h: (0, 0)),
        ],
        out_specs=pl.BlockSpec((1, S, LANES), lambda b, h: (b, 0, h)),
        out_shape=jax.ShapeDtypeStruct((B, S, DIFF_WIDTH), BF16),
        scratch_shapes=scratch,
        compiler_params=pltpu.CompilerParams(
            dimension_semantics=("arbitrary", "arbitrary"),
            vmem_limit_bytes=_vmem_limit(vmem)),
        name="attn",
    )(jnp.asarray(_attn_schedule(nq)), proj3, proj3, proj3, bias_tiles, lam, norm_w)


def _out_ffn_kernel(x_ref, oh_ref, od_ref, ada_ref, wo_ref, g1_ref, b1_ref,
                    wg_ref, wu_ref, wd_ref, g2_ref, b2_ref, o_ref, *, alpha):
    gate_m = ada_ref[0, 2:3, :]
    shift_f = ada_ref[0, 3:4, :]
    scale_f = ada_ref[0, 4:5, :]
    gate_f = ada_ref[0, 5:6, :]
    hw = oh_ref.shape[1]
    mix = (jnp.dot(oh_ref[...], wo_ref[:hw, :], preferred_element_type=F32)
           + jnp.dot(od_ref[...], wo_ref[hw:, :], preferred_element_type=F32))
    x1 = _layer_norm_rows(alpha * x_ref[...] + (1.0 + gate_m) * mix) * g1_ref[...] + b1_ref[...]
    u = (_layer_norm_rows(x1) * (1.0 + scale_f) + shift_f).astype(BF16)
    d_ff = wg_ref.shape[1]
    y = jnp.zeros(x1.shape, F32)
    for c0 in range(0, d_ff, FF_CHUNK):
        a = jnp.dot(u, wg_ref[:, c0:c0 + FF_CHUNK], preferred_element_type=F32)
        b = jnp.dot(u, wu_ref[:, c0:c0 + FF_CHUNK], preferred_element_type=F32)
        hid = (a * (1.0 / (1.0 + jnp.exp(-a))) * b).astype(BF16)
        y = y + jnp.dot(hid, wd_ref[c0:c0 + FF_CHUNK, :], preferred_element_type=F32)
    o_ref[...] = _layer_norm_rows(alpha * x1 + (1.0 + gate_f) * y) * g2_ref[...] + b2_ref[...]


def _out_ffn(x2, oh2, od2, ada3, w_out, ln1_g, ln1_b, w_gate, w_up, w_down, ln2_g, ln2_b,
             *, seq, alpha):
    N, D = x2.shape
    d_ff = w_gate.shape[1]
    tm = ROW_TILE
    per_batch = seq // tm
    row = lambda w: pl.BlockSpec((tm, w), lambda i: (i, 0))
    const = lambda shape: pl.BlockSpec(shape, lambda i: (0, 0), pipeline_mode=pl.Buffered(1))
    vec = pl.BlockSpec((1, D), lambda i: (0, 0))
    weights = (w_out.size + w_gate.size + w_up.size + w_down.size) * 2
    vmem = (weights + 2 * 2 * tm * D * 4 + 2 * 2 * tm * oh2.shape[1] * 2
            + 6 * tm * D * 4 + 4 * tm * FF_CHUNK * 4 + (6 << 20))
    return pl.pallas_call(
        functools.partial(_out_ffn_kernel, alpha=alpha),
        grid=(N // tm,),
        in_specs=[
            row(D), row(oh2.shape[1]), row(od2.shape[1]),
            pl.BlockSpec((1, ada3.shape[1], D), lambda i: (i // per_batch, 0, 0)),
            const(w_out.shape), vec, vec,
            const(w_gate.shape), const(w_up.shape), const(w_down.shape), vec, vec,
        ],
        out_specs=row(D),
        out_shape=jax.ShapeDtypeStruct((N, D), F32),
        compiler_params=pltpu.CompilerParams(
            dimension_semantics=("arbitrary",), vmem_limit_bytes=_vmem_limit(vmem)),
        name="out_ffn",
    )(x2, oh2, od2, ada3, w_out, ln1_g, ln1_b, w_gate, w_up, w_down, ln2_g, ln2_b)


def kernel(x, c, w_ada, b_ada, w_in, lb_logits, hgrn_norm_w, lam_q1, lam_k1, lam_q2, lam_k2,
           diff_norm_w, rel_bias, w_out, ln1_g, ln1_b, w_gate, w_up, w_down, ln2_g, ln2_b):
    B, S, D = x.shape
    depth = w_ada.shape[0]
    assert S % ROW_TILE == 0 and S % HGRN_TILE == 0 and S % ATT_BLOCK == 0
    assert HGRN_TILE % (CHUNK * HGRN_GROUP) == 0 and w_in.shape[2] % PROJ_CHUNK == 0
    assert w_gate.shape[2] % FF_CHUNK == 0
    alpha = (2.0 * depth) ** 0.25
    bias_tiles = _bias_tiles(rel_bias, block=ATT_BLOCK, n_tiles=_num_near_tiles(ATT_BLOCK) + 1)
    col_scales = ((0, HGRN_WIDTH, HGRN_DK ** -0.5),
                  (HGRN_WIDTH, 2 * HGRN_WIDTH, 0.5),
                  (3 * HGRN_WIDTH, 4 * HGRN_WIDTH, 0.5),
                  (4 * HGRN_WIDTH, 4 * HGRN_WIDTH + DIFF_WIDTH, LOG2E * DIFF_DH ** -0.5))
    for l in range(depth):
        lam_init = 0.8 - 0.6 * math.exp(-0.3 * l)
        ada, lb, lam = _prep(c, w_ada[l], b_ada[l][None, :], lb_logits,
                             lam_q1[l][None, :], lam_k1[l][None, :],
                             lam_q2[l][None, :], lam_k2[l][None, :],
                             layer=l, lam_init=lam_init)
        ada3 = ada.reshape(B, 6, D)
        x2 = x.reshape(B * S, D)
        proj = _in_proj(x2, ada3, w_in[l].astype(BF16), seq=S, col_scales=col_scales)
        proj3 = proj.reshape(B, S, proj.shape[1])
        o_h = _hgrn(proj3, lb, hgrn_norm_w[l][None, :])
        o_d = _attn(proj3, bias_tiles, lam, diff_norm_w[l][None, :],
                    out_scale=1.0 - lam_init)
        out = _out_ffn(x2, o_h.reshape(B * S, -1), o_d.reshape(B * S, -1), ada3,
                       w_out[l].astype(BF16), ln1_g[l][None, :], ln1_b[l][None, :],
                       w_gate[l].astype(BF16), w_up[l].astype(BF16), w_down[l].astype(BF16),
                       ln2_g[l][None, :], ln2_b[l][None, :], seq=S, alpha=alpha)
        x = out.reshape(B, S, D)
    return x
```

```python
import functools
import math

import jax
import jax.numpy as jnp
import numpy as np
from jax import lax
from jax.experimental import pallas as pl
from jax.experimental.pallas import tpu as pltpu

HGRN_HEADS = 4
HGRN_DK = 128
HGRN_DV = 128
HGRN_WIDTH = HGRN_HEADS * HGRN_DV
DIFF_HEADS = 4
DIFF_DH = 64
DIFF_DV = 2 * DIFF_DH
DIFF_WIDTH = DIFF_HEADS * DIFF_DV
N_BUCKETS = 32
MAX_DISTANCE = 128
CHUNK = 64
LN_EPS = 1e-5
RMS_EPS = 1e-6
LOG2E = math.log2(math.e)

LANES = 128
V7X_VMEM_BYTES = 64 * 1024 * 1024

ROW_TILE = 512
HGRN_TILE = 512
HGRN_GROUP = 4
ATT_BLOCK = 512
FF_CHUNK = 256
PROJ_CHUNK = 512
MASK_VALUE = -1e30

BF16 = jnp.bfloat16
F32 = jnp.float32


def _vmem_limit(block_bytes):
    return int(min(block_bytes, V7X_VMEM_BYTES - 4 * 1024 * 1024))


def _layer_norm_rows(x):
    mu = jnp.mean(x, axis=-1, keepdims=True)
    xc = x - mu
    var = jnp.mean(xc * xc, axis=-1, keepdims=True)
    return xc * lax.rsqrt(var + LN_EPS)


def _dot_nt(a, b):
    return lax.dot_general(a, b, (((1,), (1,)), ((), ())), preferred_element_type=F32)


def _dot_tn(a, b):
    return lax.dot_general(a, b, (((0,), (0,)), ((), ())), preferred_element_type=F32)


def _prep_kernel(c_ref, w_ref, b_ref, lbl_ref, q1_ref, k1_ref, q2_ref, k2_ref,
                 ada_ref, lb_ref, lam_ref, *, layer, lam_init):
    c = c_ref[...]
    sc = c * (1.0 / (1.0 + jnp.exp(-c)))
    ada_ref[...] = jnp.dot(sc.astype(BF16), w_ref[...].astype(BF16),
                           preferred_element_type=F32) + b_ref[...]

    @pl.when(pl.program_id(0) == 0)
    def _():
        n_rows = lbl_ref.shape[0]
        rows = [lbl_ref[r:r + 1, :] for r in range(n_rows)]
        mx = functools.reduce(jnp.maximum, rows)
        es = [jnp.exp(r - mx) for r in rows]
        den = functools.reduce(lambda a, b: a + b, es)
        num = functools.reduce(lambda a, b: a + b, es[:layer + 1])
        lb_ref[...] = num / den
        s1 = jnp.sum(q1_ref[...] * k1_ref[...], axis=-1, keepdims=True)
        s2 = jnp.sum(q2_ref[...] * k2_ref[...], axis=-1, keepdims=True)
        lam = jnp.exp(s1) - jnp.exp(s2) + lam_init
        lam_ref[...] = jnp.broadcast_to(lam, lam_ref.shape)


def _prep(c, w_ada, b_ada, lb_logits, lam_q1, lam_k1, lam_q2, lam_k2, *, layer, lam_init):
    B, D = c.shape
    n_out = w_ada.shape[1]
    col = D
    width = lb_logits.shape[1]
    small = lambda shape: pl.BlockSpec(shape, lambda j: (0,) * len(shape))
    return pl.pallas_call(
        functools.partial(_prep_kernel, layer=layer, lam_init=lam_init),
        grid=(n_out // col,),
        in_specs=[
            small((B, D)),
            pl.BlockSpec((D, col), lambda j: (0, j)),
            pl.BlockSpec((1, col), lambda j: (0, j)),
            small(lb_logits.shape),
            small((1, DIFF_DH)), small((1, DIFF_DH)), small((1, DIFF_DH)), small((1, DIFF_DH)),
        ],
        out_specs=[
            pl.BlockSpec((B, col), lambda j: (0, j)),
            small((1, width)),
            small((1, LANES)),
        ],
        out_shape=[
            jax.ShapeDtypeStruct((B, n_out), F32),
            jax.ShapeDtypeStruct((1, width), F32),
            jax.ShapeDtypeStruct((1, LANES), F32),
        ],
        compiler_params=pltpu.CompilerParams(
            dimension_semantics=("arbitrary",),
            vmem_limit_bytes=_vmem_limit(4 * D * col * 4 + (8 << 20))),
        name="prep",
    )(c, w_ada, b_ada, lb_logits, lam_q1, lam_k1, lam_q2, lam_k2)


def _t5_bucket_np(dist):
    max_exact = N_BUCKETS // 2
    d = np.maximum(dist, 1).astype(np.float32)
    large = max_exact + (np.log(d / max_exact) / math.log(MAX_DISTANCE / max_exact)
                         * (N_BUCKETS - max_exact)).astype(np.int32)
    large = np.minimum(large, N_BUCKETS - 1)
    return np.where(dist < max_exact, dist, large)


def _bias_reach():
    first_last = int(np.argmax(_t5_bucket_np(np.arange(0, 4 * MAX_DISTANCE)) == N_BUCKETS - 1))
    assert np.all(_t5_bucket_np(np.arange(first_last, 64 * MAX_DISTANCE)) == N_BUCKETS - 1)
    return first_last


def _num_near_tiles(block):
    t = 0
    while t * block - (block - 1) < _bias_reach():
        t += 1
    return t


def _bias_kernel(tab_ref, o_ref, *, block):
    h = pl.program_id(0)
    t = pl.program_id(1)
    r = lax.broadcasted_iota(jnp.int32, (block, block), 0)
    c = lax.broadcasted_iota(jnp.int32, (block, block), 1)
    dist = r - c + t * block
    rel = jnp.maximum(dist, 0)
    max_exact = N_BUCKETS // 2
    d = jnp.maximum(rel, 1).astype(F32)
    large = max_exact + (jnp.log(d / max_exact) / math.log(MAX_DISTANCE / max_exact)
                         * (N_BUCKETS - max_exact)).astype(jnp.int32)
    large = jnp.minimum(large, N_BUCKETS - 1)
    bucket = jnp.where(rel < max_exact, rel, large)
    bias = jnp.zeros((block, block), F32)
    for b in range(N_BUCKETS):
        bias = jnp.where(bucket == b, tab_ref[b, h], bias)
    shifted = (bias - tab_ref[N_BUCKETS - 1, h]) * LOG2E
    o_ref[0, 0] = jnp.where(dist >= 0, shifted, MASK_VALUE)


def _bias_tiles(rel_bias, *, block, n_tiles):
    H = rel_bias.shape[1]
    return pl.pallas_call(
        functools.partial(_bias_kernel, block=block),
        grid=(H, n_tiles),
        in_specs=[pl.BlockSpec(memory_space=pltpu.SMEM)],
        out_specs=pl.BlockSpec((1, 1, block, block), lambda h, t: (h, t, 0, 0)),
        out_shape=jax.ShapeDtypeStruct((H, n_tiles, block, block), F32),
        compiler_params=pltpu.CompilerParams(dimension_semantics=("arbitrary", "arbitrary")),
        name="bias_tiles",
    )(rel_bias)


def _in_proj_kernel(x_ref, ada_ref, w_ref, o_ref, *, col_scales):
    x = x_ref[...]
    shift = ada_ref[0, 0:1, :]
    scale = ada_ref[0, 1:2, :]
    u = (_layer_norm_rows(x) * (1.0 + scale) + shift).astype(BF16)
    n_out = o_ref.shape[1]
    for c0 in range(0, n_out, PROJ_CHUNK):
        acc = jnp.dot(u, w_ref[:, c0:c0 + PROJ_CHUNK], preferred_element_type=F32)
        for lo, hi, s in col_scales:
            if lo <= c0 < hi:
                acc = acc * s
        o_ref[:, c0:c0 + PROJ_CHUNK] = acc.astype(o_ref.dtype)


def _in_proj(x2, ada3, w_in, *, seq, col_scales):
    N, D = x2.shape
    n_out = w_in.shape[1]
    tm = ROW_TILE
    per_batch = seq // tm
    assert all(lo % PROJ_CHUNK == 0 and hi % PROJ_CHUNK == 0 for lo, hi, _ in col_scales)
    vmem = (2 * tm * D * 4 + D * n_out * 2 + 2 * tm * n_out * 2 + 2 * 8 * D * 4
            + 3 * tm * D * 4 + 2 * tm * PROJ_CHUNK * 4 + (4 << 20))
    return pl.pallas_call(
        functools.partial(_in_proj_kernel, col_scales=col_scales),
        grid=(N // tm,),
        in_specs=[
            pl.BlockSpec((tm, D), lambda i: (i, 0)),
            pl.BlockSpec((1, ada3.shape[1], D), lambda i: (i // per_batch, 0, 0)),
            pl.BlockSpec((D, n_out), lambda i: (0, 0), pipeline_mode=pl.Buffered(1)),
        ],
        out_specs=pl.BlockSpec((tm, n_out), lambda i: (i, 0)),
        out_shape=jax.ShapeDtypeStruct((N, n_out), BF16),
        compiler_params=pltpu.CompilerParams(
            dimension_semantics=("arbitrary",), vmem_limit_bytes=_vmem_limit(vmem)),
        name="in_proj",
    )(x2, ada3, w_in)


def _decay_matrix():
    n = 2 * CHUNK + 16
    r = lax.broadcasted_iota(jnp.int32, (n, CHUNK), 0)
    s = lax.broadcasted_iota(jnp.int32, (n, CHUNK), 1)
    upto_mid = (s < CHUNK // 2).astype(F32)
    d1 = (s <= r).astype(F32) - upto_mid
    d2 = (s > r - CHUNK).astype(F32)
    return jnp.where(r < CHUNK, d1, jnp.where(r < 2 * CHUNK, d2,
                     jnp.where(r < 2 * CHUNK + 8, upto_mid, 1.0))).astype(BF16)


def _split3(x):
    hi = x.astype(BF16)
    r1 = x - hi.astype(F32)
    mid = r1.astype(BF16)
    lo = (r1 - mid.astype(F32)).astype(BF16)
    return hi, mid, lo


def _hgrn_kernel(q_ref, f_ref, i_ref, g_ref, lb_ref, nw_ref, o_ref,
                 st_ref, qd_ref, kd_ref, qin_ref, kst_ref, dl_ref, of_ref):
    @pl.when(pl.program_id(1) == 0)
    def _():
        st_ref[...] = jnp.zeros_like(st_ref)

    lb = lb_ref[...]
    one_m_lb = 1.0 - lb
    nw = nw_ref[...]
    n_chunks = q_ref.shape[1] // CHUNK
    tri_r = lax.broadcasted_iota(jnp.int32, (CHUNK, CHUNK), 0)
    tri_c = lax.broadcasted_iota(jnp.int32, (CHUNK, CHUNK), 1)
    causal = tri_r >= tri_c
    decay = _decay_matrix()
    decay3 = jnp.concatenate([decay, decay, decay], axis=1)
    sub = CHUNK // 8

    half_span = 0.5 * one_m_lb
    f_mid = lb + half_span

    def chunk_rows(n):
        return pl.ds(pl.multiple_of(n * CHUNK, CHUNK), CHUNK)

    def heads():
        return [slice(h * HGRN_DK, (h + 1) * HGRN_DK) for h in range(HGRN_HEADS)]


    def gates(n):
        x = f_ref[0, chunk_rows(n), :].astype(F32)
        th = half_span * jnp.tanh(x)
        lf2 = jnp.log2(f_mid + th)
        return half_span - th, jnp.concatenate(_split3(lf2), axis=0)

    def operands(n, kc, rel):
        rows = chunk_rows(n)
        q = q_ref[0, rows, :].astype(F32)
        d_mid = rel[:CHUNK]
        e1 = jnp.exp2(d_mid)
        qd_f = q * e1
        qd_ref[rows, :] = qd_f.astype(BF16)
        kd_ref[rows, :] = (kc * jnp.exp2(-d_mid)).astype(BF16)
        e_mid = jnp.exp2(rel[2 * CHUNK:2 * CHUNK + 8])
        qin_ref[rows, :] = (qd_f.reshape(sub, 8, -1) * e_mid[None]).reshape(qd_f.shape).astype(BF16)
        kst_ref[rows, :] = (kc * jnp.exp2(rel[CHUNK:2 * CHUNK])).astype(BF16)
        dl_ref[pl.ds(pl.multiple_of(n * 8, 8), 8), :] = jnp.exp2(rel[2 * CHUNK + 8:])

    def prep_group(i, carry):
        ns = [i * HGRN_GROUP + j for j in range(HGRN_GROUP)]
        g1 = [gates(n) for n in ns]
        rels = [jnp.dot(decay3, parts, preferred_element_type=F32) for _, parts in g1]
        for n, (kc, _), rel in zip(ns, g1, rels):
            operands(n, kc, rel)
        return carry

    lax.fori_loop(0, n_chunks // HGRN_GROUP, prep_group, 0)

    def mix_group(i, carry):
        ns = [i * HGRN_GROUP + j for j in range(HGRN_GROUP)]
        local = []
        for n in ns:
            rows = chunk_rows(n)
            v = i_ref[0, rows, :]
            per_head = []
            for sl in heads():
                a = jnp.where(causal, _dot_nt(qd_ref[rows, sl], kd_ref[rows, sl]), 0.0)
                per_head.append((a.astype(BF16), _dot_tn(v[:, sl], kst_ref[rows, sl])))
            local.append((rows, v, per_head))
        for n, (rows, v, per_head) in zip(ns, local):
            d_last = dl_ref[pl.ds(pl.multiple_of(n * 8, 8), 8), :]
            for h, (sl, (a, upd)) in enumerate(zip(heads(), per_head)):
                st = st_ref[h]
                of_ref[rows, sl] = (jnp.dot(a, v[:, sl], preferred_element_type=F32)
                                    + _dot_nt(qin_ref[rows, sl], st.astype(BF16)))
                st_new = st.reshape(HGRN_DV // 8, 8, HGRN_DK) * d_last[None, :, sl]
                st_ref[h] = st_new.reshape(HGRN_DV, HGRN_DK) + upd
        return carry

    lax.fori_loop(0, n_chunks // HGRN_GROUP, mix_group, 0)

    def norm_block(n, carry):
        rows = chunk_rows(n)
        hg = g_ref[0, rows, :].astype(F32)
        gate = hg + hg * jnp.tanh(hg)
        for sl in heads():
            o = of_ref[rows, sl]
            ms = jnp.mean(o * o, axis=-1, keepdims=True)
            o_ref[0, rows, sl] = (o * lax.rsqrt(ms + RMS_EPS) * nw * gate[:, sl]).astype(o_ref.dtype)
        return carry

    lax.fori_loop(0, n_chunks, norm_block, 0, unroll=2)


def _hgrn(proj3, lb, norm_w):
    B, S, _ = proj3.shape
    ts = HGRN_TILE
    W = HGRN_WIDTH
    spec = lambda j: pl.BlockSpec((1, ts, W), lambda b, s, j=j: (b, s, j))
    vmem = 5 * 2 * ts * W * 2 + HGRN_HEADS * HGRN_DK * HGRN_DV * 4 + (16 << 20)
    return pl.pallas_call(
        _hgrn_kernel,
        grid=(B, S // ts),
        in_specs=[spec(0), spec(1), spec(2), spec(3),
                  pl.BlockSpec((1, W), lambda b, s: (0, 0)),
                  pl.BlockSpec((1, HGRN_DV), lambda b, s: (0, 0))],
        out_specs=pl.BlockSpec((1, ts, W), lambda b, s: (b, s, 0)),
        out_shape=jax.ShapeDtypeStruct((B, S, W), BF16),
        scratch_shapes=[pltpu.VMEM((HGRN_HEADS, HGRN_DV, HGRN_DK), F32),
                        pltpu.VMEM((ts, W), BF16), pltpu.VMEM((ts, W), BF16),
                        pltpu.VMEM((ts, W), BF16), pltpu.VMEM((ts, W), BF16),
                        pltpu.VMEM((ts // CHUNK * 8, W), F32),
                        pltpu.VMEM((ts, W), F32)],
        compiler_params=pltpu.CompilerParams(
            dimension_semantics=("arbitrary", "arbitrary"), vmem_limit_bytes=_vmem_limit(vmem)),
        name="hgrn",
    )(proj3, proj3, proj3, proj3, lb, norm_w)


def _attn_schedule(nq):
    pairs = [(qi, kj) for qi in range(1, nq) for kj in range(qi)]
    nxt = pairs[1:] + pairs[-1:]
    return np.array([[p[0] for p in nxt], [p[1] for p in nxt],
                     [p[0] for p in pairs], [p[1] for p in pairs]], np.int32)


def _attn_kernel(sched_ref, q_ref, k_ref, v_ref, diag_ref, near_ref, lam_ref, nw_ref, o_ref,
                 s0_ref, s1_ref, m_ref, acc_ref, *, block, out_scale):
    tq = block
    half = tq // 2
    nq = q_ref.shape[1] // tq
    n_off = sched_ref.shape[1]
    ones = jnp.ones((block, LANES), BF16)
    lam = lam_ref[...]

    def blk(j):
        return pl.ds(pl.multiple_of(j * block, block), block)

    def stacked_queries(qi):
        q12 = q_ref[0, blk(qi), :]
        lane = lax.broadcasted_iota(jnp.int32, q12.shape, 1)
        zero = jnp.zeros_like(q12)
        q1 = jnp.where(lane < DIFF_DH, q12, zero)
        q2 = jnp.where(lane >= DIFF_DH, q12, zero)
        return jnp.concatenate([q1[:half], q2[:half], q1[half:], q2[half:]], axis=0)

    def values(kj):
        return jnp.concatenate([v_ref[0, blk(kj), :], ones], axis=1)

    def online_softmax(s, rows, qi, v_ext):
        n_lane_tiles = s.shape[1] // LANES
        mx = s[:, :LANES]
        for c in range(1, n_lane_tiles):
            mx = jnp.maximum(mx, s[:, c * LANES:(c + 1) * LANES])
        m_prev = m_ref[qi, rows, :]
        m_new = jnp.maximum(m_prev, jnp.max(mx, axis=-1, keepdims=True))
        alpha = jnp.exp2(m_prev - m_new)
        p = jnp.exp2(s - jnp.concatenate([m_new] * n_lane_tiles, axis=1)).astype(BF16)
        pv = jnp.dot(p, v_ext, preferred_element_type=F32)
        acc_ref[qi, rows, :] = jnp.concatenate([alpha, alpha], axis=1) * acc_ref[qi, rows, :] + pv
        m_ref[qi, rows, :] = m_new

    @pl.when((pl.program_id(0) == 0) & (pl.program_id(1) == 0))
    def _():
        m_ref[...] = jnp.full_like(m_ref, MASK_VALUE)
        acc_ref[...] = jnp.zeros_like(acc_ref)

    def off_scores(qi, kj, s_ref):
        s_ref[...] = _dot_nt(stacked_queries(qi), k_ref[0, blk(kj), :])

    def off_update(step, s_ref):
        qi = sched_ref[2, step]
        kj = sched_ref[3, step]
        corner = jnp.where(qi - kj == 1, 1.0, 0.0) * near_ref[0, 0]
        for r0 in (0, half):
            s_ref[r0:r0 + LANES, tq - LANES:] += corner
        online_softmax(s_ref[...], slice(0, 2 * tq), qi, values(kj))

    def off_pair(i, carry):
        off_scores(sched_ref[0, 2 * i], sched_ref[1, 2 * i], s1_ref)
        off_update(2 * i, s0_ref)
        off_scores(sched_ref[0, 2 * i + 1], sched_ref[1, 2 * i + 1], s0_ref)
        off_update(2 * i + 1, s1_ref)
        return carry

    off_scores(1, 0, s0_ref)
    lax.fori_loop(0, n_off // 2, off_pair, 0)

    def diag_scores(qi, s_ref):
        qz = stacked_queries(qi)
        s_ref[:tq, :half] = _dot_nt(qz[:tq], k_ref[0, pl.ds(pl.multiple_of(qi * tq, tq), half), :])
        s_ref[tq:, :] = _dot_nt(qz[tq:], k_ref[0, blk(qi), :])

    def diag_update(qi, s_ref):
        bias = diag_ref[0, 0]
        v_ext = values(qi)
        bias_a = bias[:half, :half]
        bias_b = bias[half:, :]
        online_softmax(s_ref[:tq, :half] + jnp.concatenate([bias_a, bias_a], axis=0),
                       slice(0, tq), qi, v_ext[:half])
        online_softmax(s_ref[tq:, :] + jnp.concatenate([bias_b, bias_b], axis=0),
                       slice(tq, 2 * tq), qi, v_ext)
        acc = acc_ref[qi]
        o_all = acc[:, :DIFF_DV] / acc[:, DIFF_DV:]
        o = jnp.concatenate([o_all[:half] - lam * o_all[half:tq],
                             o_all[tq:tq + half] - lam * o_all[tq + half:]], axis=0)
        ms = jnp.mean(o * o, axis=-1, keepdims=True)
        y = o * lax.rsqrt(ms + RMS_EPS) * nw_ref[...] * out_scale
        o_ref[0, blk(qi), :] = y.astype(o_ref.dtype)
        m_ref[qi] = jnp.full((2 * tq, LANES), MASK_VALUE, F32)
        acc_ref[qi] = jnp.zeros_like(acc)

    def diag_pair(i, carry):
        diag_scores(jnp.minimum(2 * i + 1, nq - 1), s1_ref)
        diag_update(2 * i, s0_ref)
        diag_scores(jnp.minimum(2 * i + 2, nq - 1), s0_ref)
        diag_update(2 * i + 1, s1_ref)
        return carry

    diag_scores(0, s0_ref)
    lax.fori_loop(0, nq // 2, diag_pair, 0)


def _attn(proj3, bias_tiles, lam, norm_w, *, out_scale):
    B, S, _ = proj3.shape
    H = DIFF_HEADS
    block = ATT_BLOCK
    nq = S // block
    sched = _attn_schedule(nq)
    assert nq % 2 == 0 and sched.shape[1] % 2 == 0, "pipeline steps are unrolled in pairs"
    assert bias_tiles.shape[1] == 2 and _bias_reach() <= LANES <= block // 2
    q_blk = 4 * HGRN_HEADS
    k_blk = q_blk + H
    v_blk = k_blk + H
    seq_spec = lambda j: pl.BlockSpec((1, S, LANES), lambda b, h, j=j: (b, 0, j + h))
    rows = 2 * block
    scratch = [pltpu.VMEM((rows, block), F32), pltpu.VMEM((rows, block), F32),
               pltpu.VMEM((nq, rows, LANES), F32),
               pltpu.VMEM((nq, rows, 2 * LANES), F32)]
    scratch_bytes = rows * (2 * block * 4 + nq * 3 * LANES * 4)
    vmem = (4 * 2 * S * LANES * 2 + 2 * block * block * 4 + scratch_bytes
            + 5 * rows * block * 4 + (6 << 20))
    return pl.pallas_call(
        functools.partial(_attn_kernel, block=block, out_scale=out_scale),
        grid=(B, H),
        in_specs=[
            pl.BlockSpec(memory_space=pltpu.SMEM),
            seq_spec(q_blk), seq_spec(k_blk), seq_spec(v_blk),
            pl.BlockSpec((1, 1, block, block), lambda b, h: (h, 0, 0, 0)),
            pl.BlockSpec((1, 1, LANES, LANES), lambda b, h: (h, 1, 0, block // LANES - 1)),
            pl.BlockSpec((1, LANES), lambda b, h: (0, 0)),
            pl.BlockSpec((1, DIFF_DV), lambda b, h: (0, 0)),
        ],
        out_specs=pl.BlockSpec((1, S, LANES), lambda b, h: (b, 0, h)),
        out_shape=jax.ShapeDtypeStruct((B, S, DIFF_WIDTH), BF16),
        scratch_shapes=scratch,
        compiler_params=pltpu.CompilerParams(
            dimension_semantics=("arbitrary", "arbitrary"),
            vmem_limit_bytes=_vmem_limit(vmem)),
        name="attn",
    )(jnp.asarray(sched), proj3, proj3, proj3, bias_tiles, bias_tiles, lam, norm_w)


def _out_ffn_kernel(x_ref, oh_ref, od_ref, ada_ref, wo_ref, g1_ref, b1_ref,
                    wg_ref, wu_ref, wd_ref, g2_ref, b2_ref, o_ref, *, alpha):
    gate_m = ada_ref[0, 2:3, :]
    shift_f = ada_ref[0, 3:4, :]
    scale_f = ada_ref[0, 4:5, :]
    gate_f = ada_ref[0, 5:6, :]
    hw = oh_ref.shape[1]
    mix = (jnp.dot(oh_ref[...], wo_ref[:hw, :], preferred_element_type=F32)
           + jnp.dot(od_ref[...], wo_ref[hw:, :], preferred_element_type=F32))
    x1 = _layer_norm_rows(alpha * x_ref[...] + (1.0 + gate_m) * mix) * g1_ref[...] + b1_ref[...]
    u = (_layer_norm_rows(x1) * (1.0 + scale_f) + shift_f).astype(BF16)
    d_ff = wg_ref.shape[1]
    y = jnp.zeros(x1.shape, F32)
    for c0 in range(0, d_ff, FF_CHUNK):
        a = jnp.dot(u, wg_ref[:, c0:c0 + FF_CHUNK], preferred_element_type=F32)
        b = jnp.dot(u, wu_ref[:, c0:c0 + FF_CHUNK], preferred_element_type=F32)
        hid = (a * (1.0 / (1.0 + jnp.exp(-a))) * b).astype(BF16)
        y = y + jnp.dot(hid, wd_ref[c0:c0 + FF_CHUNK, :], preferred_element_type=F32)
    o_ref[...] = _layer_norm_rows(alpha * x1 + (1.0 + gate_f) * y) * g2_ref[...] + b2_ref[...]


def _out_ffn(x2, oh2, od2, ada3, w_out, ln1_g, ln1_b, w_gate, w_up, w_down, ln2_g, ln2_b,
             *, seq, alpha):
    N, D = x2.shape
    d_ff = w_gate.shape[1]
    tm = ROW_TILE
    per_batch = seq // tm
    row = lambda w: pl.BlockSpec((tm, w), lambda i: (i, 0))
    const = lambda shape: pl.BlockSpec(shape, lambda i: (0, 0), pipeline_mode=pl.Buffered(1))
    vec = pl.BlockSpec((1, D), lambda i: (0, 0))
    weights = (w_out.size + w_gate.size + w_up.size + w_down.size) * 2
    vmem = (weights + 2 * 2 * tm * D * 4 + 2 * 2 * tm * oh2.shape[1] * 2
            + 6 * tm * D * 4 + 4 * tm * FF_CHUNK * 4 + (6 << 20))
    return pl.pallas_call(
        functools.partial(_out_ffn_kernel, alpha=alpha),
        grid=(N // tm,),
        in_specs=[
            row(D), row(oh2.shape[1]), row(od2.shape[1]),
            pl.BlockSpec((1, ada3.shape[1], D), lambda i: (i // per_batch, 0, 0)),
            const(w_out.shape), vec, vec,
            const(w_gate.shape), const(w_up.shape), const(w_down.shape), vec, vec,
        ],
        out_specs=row(D),
        out_shape=jax.ShapeDtypeStruct((N, D), F32),
        compiler_params=pltpu.CompilerParams(
            dimension_semantics=("arbitrary",), vmem_limit_bytes=_vmem_limit(vmem)),
        name="out_ffn",
    )(x2, oh2, od2, ada3, w_out, ln1_g, ln1_b, w_gate, w_up, w_down, ln2_g, ln2_b)


def kernel(x, c, w_ada, b_ada, w_in, lb_logits, hgrn_norm_w, lam_q1, lam_k1, lam_q2, lam_k2,
           diff_norm_w, rel_bias, w_out, ln1_g, ln1_b, w_gate, w_up, w_down, ln2_g, ln2_b):
    B, S, D = x.shape
    depth = w_ada.shape[0]
    assert S % ROW_TILE == 0 and S % HGRN_TILE == 0 and S % ATT_BLOCK == 0
    assert HGRN_TILE % (CHUNK * HGRN_GROUP) == 0 and w_in.shape[2] % PROJ_CHUNK == 0
    assert w_gate.shape[2] % FF_CHUNK == 0
    alpha = (2.0 * depth) ** 0.25
    bias_tiles = _bias_tiles(rel_bias, block=ATT_BLOCK, n_tiles=_num_near_tiles(ATT_BLOCK))
    col_scales = ((0, HGRN_WIDTH, HGRN_DK ** -0.5),
                  (HGRN_WIDTH, 2 * HGRN_WIDTH, 0.5),
                  (3 * HGRN_WIDTH, 4 * HGRN_WIDTH, 0.5),
                  (4 * HGRN_WIDTH, 4 * HGRN_WIDTH + DIFF_WIDTH, LOG2E * DIFF_DH ** -0.5))
    for l in range(depth):
        lam_init = 0.8 - 0.6 * math.exp(-0.3 * l)
        ada, lb, lam = _prep(c, w_ada[l], b_ada[l][None, :], lb_logits,
                             lam_q1[l][None, :], lam_k1[l][None, :],
                             lam_q2[l][None, :], lam_k2[l][None, :],
                             layer=l, lam_init=lam_init)
        ada3 = ada.reshape(B, 6, D)
        x2 = x.reshape(B * S, D)
        proj = _in_proj(x2, ada3, w_in[l].astype(BF16), seq=S, col_scales=col_scales)
        proj3 = proj.reshape(B, S, proj.shape[1])
        o_h = _hgrn(proj3, lb, hgrn_norm_w[l][None, :])
        o_d = _attn(proj3, bias_tiles, lam, diff_norm_w[l][None, :],
                    out_scale=1.0 - lam_init)
        out = _out_ffn(x2, o_h.reshape(B * S, -1), o_d.reshape(B * S, -1), ada3,
                       w_out[l].astype(BF16), ln1_g[l][None, :], ln1_b[l][None, :],
                       w_gate[l].astype(BF16), w_up[l].astype(BF16), w_down[l].astype(BF16),
                       ln2_g[l][None, :], ln2_b[l][None, :], seq=S, alpha=alpha)
        x = out.reshape(B, S, D)
    return x
```

```python
import functools
import math

import jax
import jax.numpy as jnp
import numpy as np
from jax import lax
from jax.experimental import pallas as pl
from jax.experimental.pallas import tpu as pltpu

HGRN_HEADS = 4
HGRN_DK = 128
HGRN_DV = 128
HGRN_WIDTH = HGRN_HEADS * HGRN_DV
DIFF_HEADS = 4
DIFF_DH = 64
DIFF_DV = 2 * DIFF_DH
DIFF_WIDTH = DIFF_HEADS * DIFF_DV
N_BUCKETS = 32
MAX_DISTANCE = 128
CHUNK = 64
LN_EPS = 1e-5
RMS_EPS = 1e-6
LOG2E = math.log2(math.e)

LANES = 128
V7X_VMEM_BYTES = 64 * 1024 * 1024

ROW_TILE = 512
HGRN_TILE = 512
HGRN_GROUP = 4
ATT_BLOCK = 512
FF_CHUNK = 256
FFN_SUBTILES = 2
PROJ_SUBTILES = 2
PROJ_CHUNK = 512
MASK_VALUE = -1e30

BF16 = jnp.bfloat16
F32 = jnp.float32


def _vmem_limit(block_bytes):
    return int(min(block_bytes, V7X_VMEM_BYTES - 4 * 1024 * 1024))


def _layer_norm_rows(x):
    mu = jnp.mean(x, axis=-1, keepdims=True)
    xc = x - mu
    var = jnp.mean(xc * xc, axis=-1, keepdims=True)
    return xc * lax.rsqrt(var + LN_EPS)


def _dot_nt(a, b):
    return lax.dot_general(a, b, (((1,), (1,)), ((), ())), preferred_element_type=F32)


def _dot_tn(a, b):
    return lax.dot_general(a, b, (((0,), (0,)), ((), ())), preferred_element_type=F32)


def _prep_kernel(c_ref, w_ref, b_ref, lbl_ref, q1_ref, k1_ref, q2_ref, k2_ref,
                 ada_ref, lb_ref, lam_ref, *, layer, lam_init):
    c = c_ref[...]
    sc = c * (1.0 / (1.0 + jnp.exp(-c)))
    ada_ref[...] = jnp.dot(sc.astype(BF16), w_ref[...].astype(BF16),
                           preferred_element_type=F32) + b_ref[...]

    @pl.when(pl.program_id(0) == 0)
    def _():
        n_rows = lbl_ref.shape[0]
        rows = [lbl_ref[r:r + 1, :] for r in range(n_rows)]
        mx = functools.reduce(jnp.maximum, rows)
        es = [jnp.exp(r - mx) for r in rows]
        den = functools.reduce(lambda a, b: a + b, es)
        num = functools.reduce(lambda a, b: a + b, es[:layer + 1])
        lb_ref[...] = num / den
        s1 = jnp.sum(q1_ref[...] * k1_ref[...], axis=-1, keepdims=True)
        s2 = jnp.sum(q2_ref[...] * k2_ref[...], axis=-1, keepdims=True)
        lam = jnp.exp(s1) - jnp.exp(s2) + lam_init
        lam_ref[...] = jnp.broadcast_to(lam, lam_ref.shape)


def _prep(c, w_ada, b_ada, lb_logits, lam_q1, lam_k1, lam_q2, lam_k2, *, layer, lam_init):
    B, D = c.shape
    n_out = w_ada.shape[1]
    col = D
    width = lb_logits.shape[1]
    small = lambda shape: pl.BlockSpec(shape, lambda j: (0,) * len(shape))
    return pl.pallas_call(
        functools.partial(_prep_kernel, layer=layer, lam_init=lam_init),
        grid=(n_out // col,),
        in_specs=[
            small((B, D)),
            pl.BlockSpec((D, col), lambda j: (0, j)),
            pl.BlockSpec((1, col), lambda j: (0, j)),
            small(lb_logits.shape),
            small((1, DIFF_DH)), small((1, DIFF_DH)), small((1, DIFF_DH)), small((1, DIFF_DH)),
        ],
        out_specs=[
            pl.BlockSpec((B, col), lambda j: (0, j)),
            small((1, width)),
            small((1, LANES)),
        ],
        out_shape=[
            jax.ShapeDtypeStruct((B, n_out), F32),
            jax.ShapeDtypeStruct((1, width), F32),
            jax.ShapeDtypeStruct((1, LANES), F32),
        ],
        compiler_params=pltpu.CompilerParams(
            dimension_semantics=("arbitrary",),
            vmem_limit_bytes=_vmem_limit(4 * D * col * 4 + (8 << 20))),
        name="prep",
    )(c, w_ada, b_ada, lb_logits, lam_q1, lam_k1, lam_q2, lam_k2)


def _t5_bucket_np(dist):
    max_exact = N_BUCKETS // 2
    d = np.maximum(dist, 1).astype(np.float32)
    large = max_exact + (np.log(d / max_exact) / math.log(MAX_DISTANCE / max_exact)
                         * (N_BUCKETS - max_exact)).astype(np.int32)
    large = np.minimum(large, N_BUCKETS - 1)
    return np.where(dist < max_exact, dist, large)


def _bias_reach():
    first_last = int(np.argmax(_t5_bucket_np(np.arange(0, 4 * MAX_DISTANCE)) == N_BUCKETS - 1))
    assert np.all(_t5_bucket_np(np.arange(first_last, 64 * MAX_DISTANCE)) == N_BUCKETS - 1)
    return first_last


def _num_near_tiles(block):
    t = 0
    while t * block - (block - 1) < _bias_reach():
        t += 1
    return t


def _bias_kernel(tab_ref, o_ref, *, block):
    h = pl.program_id(0)
    t = pl.program_id(1)
    r = lax.broadcasted_iota(jnp.int32, (block, block), 0)
    c = lax.broadcasted_iota(jnp.int32, (block, block), 1)
    dist = r - c + t * block
    rel = jnp.maximum(dist, 0)
    max_exact = N_BUCKETS // 2
    d = jnp.maximum(rel, 1).astype(F32)
    large = max_exact + (jnp.log(d / max_exact) / math.log(MAX_DISTANCE / max_exact)
                         * (N_BUCKETS - max_exact)).astype(jnp.int32)
    large = jnp.minimum(large, N_BUCKETS - 1)
    bucket = jnp.where(rel < max_exact, rel, large)
    bias = jnp.zeros((block, block), F32)
    for b in range(N_BUCKETS):
        bias = jnp.where(bucket == b, tab_ref[b, h], bias)
    shifted = (bias - tab_ref[N_BUCKETS - 1, h]) * LOG2E
    o_ref[0, 0] = jnp.where(dist >= 0, shifted, MASK_VALUE)


def _bias_tiles(rel_bias, *, block, n_tiles):
    H = rel_bias.shape[1]
    return pl.pallas_call(
        functools.partial(_bias_kernel, block=block),
        grid=(H, n_tiles),
        in_specs=[pl.BlockSpec(memory_space=pltpu.SMEM)],
        out_specs=pl.BlockSpec((1, 1, block, block), lambda h, t: (h, t, 0, 0)),
        out_shape=jax.ShapeDtypeStruct((H, n_tiles, block, block), F32),
        compiler_params=pltpu.CompilerParams(dimension_semantics=("arbitrary", "arbitrary")),
        name="bias_tiles",
    )(rel_bias)


def _in_proj_kernel(x_ref, ada_ref, w_ref, o_ref, *, col_scales):
    shift = ada_ref[0, 0:1, :]
    scale = ada_ref[0, 1:2, :]
    n_out = o_ref.shape[1]
    sub = x_ref.shape[0] // PROJ_SUBTILES

    def modulated(t):
        x = x_ref[t * sub:(t + 1) * sub, :]
        return (_layer_norm_rows(x) * (1.0 + scale) + shift).astype(BF16)

    u = modulated(0)
    for t in range(PROJ_SUBTILES):
        u_next = modulated(t + 1) if t + 1 < PROJ_SUBTILES else None
        for c0 in range(0, n_out, PROJ_CHUNK):
            acc = jnp.dot(u, w_ref[:, c0:c0 + PROJ_CHUNK], preferred_element_type=F32)
            for lo, hi, s in col_scales:
                if lo <= c0 < hi:
                    acc = acc * s
            o_ref[t * sub:(t + 1) * sub, c0:c0 + PROJ_CHUNK] = acc.astype(o_ref.dtype)
        u = u_next


def _in_proj(x2, ada3, w_in, *, seq, col_scales):
    N, D = x2.shape
    n_out = w_in.shape[1]
    tm = ROW_TILE * PROJ_SUBTILES
    per_batch = seq // tm
    assert all(lo % PROJ_CHUNK == 0 and hi % PROJ_CHUNK == 0 for lo, hi, _ in col_scales)
    vmem = (2 * tm * D * 4 + D * n_out * 2 + 2 * tm * n_out * 2 + 2 * 8 * D * 4
            + 3 * tm * D * 4 + 2 * tm * PROJ_CHUNK * 4 + (4 << 20))
    return pl.pallas_call(
        functools.partial(_in_proj_kernel, col_scales=col_scales),
        grid=(N // tm,),
        in_specs=[
            pl.BlockSpec((tm, D), lambda i: (i, 0)),
            pl.BlockSpec((1, ada3.shape[1], D), lambda i: (i // per_batch, 0, 0)),
            pl.BlockSpec((D, n_out), lambda i: (0, 0), pipeline_mode=pl.Buffered(1)),
        ],
        out_specs=pl.BlockSpec((tm, n_out), lambda i: (i, 0)),
        out_shape=jax.ShapeDtypeStruct((N, n_out), BF16),
        compiler_params=pltpu.CompilerParams(
            dimension_semantics=("arbitrary",), vmem_limit_bytes=_vmem_limit(vmem)),
        name="in_proj",
    )(x2, ada3, w_in)


def _decay_matrix():
    n = 2 * CHUNK + 16
    r = lax.broadcasted_iota(jnp.int32, (n, CHUNK), 0)
    s = lax.broadcasted_iota(jnp.int32, (n, CHUNK), 1)
    upto_mid = (s < CHUNK // 2).astype(F32)
    d1 = (s <= r).astype(F32) - upto_mid
    d2 = (s > r - CHUNK).astype(F32)
    return jnp.where(r < CHUNK, d1, jnp.where(r < 2 * CHUNK, d2,
                     jnp.where(r < 2 * CHUNK + 8, upto_mid, 1.0))).astype(BF16)


def _split3(x):
    hi = x.astype(BF16)
    r1 = x - hi.astype(F32)
    mid = r1.astype(BF16)
    lo = (r1 - mid.astype(F32)).astype(BF16)
    return hi, mid, lo


def _hgrn_kernel(q_ref, f_ref, i_ref, g_ref, lb_ref, nw_ref, o_ref,
                 st_ref, qd_ref, kd_ref, qin_ref, kst_ref, dl_ref, of_ref):
    @pl.when(pl.program_id(1) == 0)
    def _():
        st_ref[...] = jnp.zeros_like(st_ref)

    lb = lb_ref[...]
    one_m_lb = 1.0 - lb
    nw = nw_ref[...]
    n_chunks = q_ref.shape[1] // CHUNK
    tri_r = lax.broadcasted_iota(jnp.int32, (CHUNK, CHUNK), 0)
    tri_c = lax.broadcasted_iota(jnp.int32, (CHUNK, CHUNK), 1)
    causal = tri_r >= tri_c
    decay = _decay_matrix()
    decay3 = jnp.concatenate([decay, decay, decay], axis=1)
    sub = CHUNK // 8

    half_span = 0.5 * one_m_lb
    f_mid = lb + half_span

    def chunk_rows(n):
        return pl.ds(pl.multiple_of(n * CHUNK, CHUNK), CHUNK)

    def heads():
        return [slice(h * HGRN_DK, (h + 1) * HGRN_DK) for h in range(HGRN_HEADS)]


    def gates(n):
        x = f_ref[0, chunk_rows(n), :].astype(F32)
        th = half_span * jnp.tanh(x)
        lf2 = jnp.log2(f_mid + th)
        return half_span - th, jnp.concatenate(_split3(lf2), axis=0)

    def operands(n, kc, rel):
        rows = chunk_rows(n)
        q = q_ref[0, rows, :].astype(F32)
        d_mid = rel[:CHUNK]
        e1 = jnp.exp2(d_mid)
        qd_f = q * e1
        qd_ref[rows, :] = qd_f.astype(BF16)
        kd_ref[rows, :] = (kc * jnp.exp2(-d_mid)).astype(BF16)
        e_mid = jnp.exp2(rel[2 * CHUNK:2 * CHUNK + 8])
        qin_ref[rows, :] = (qd_f.reshape(sub, 8, -1) * e_mid[None]).reshape(qd_f.shape).astype(BF16)
        kst_ref[rows, :] = (kc * jnp.exp2(rel[CHUNK:2 * CHUNK])).astype(BF16)
        dl_ref[pl.ds(pl.multiple_of(n * 8, 8), 8), :] = jnp.exp2(rel[2 * CHUNK + 8:])

    def prep_group(i, carry):
        ns = [i * HGRN_GROUP + j for j in range(HGRN_GROUP)]
        g1 = [gates(n) for n in ns]
        rels = [jnp.dot(decay3, parts, preferred_element_type=F32) for _, parts in g1]
        for n, (kc, _), rel in zip(ns, g1, rels):
            operands(n, kc, rel)
        return carry

    lax.fori_loop(0, n_chunks // HGRN_GROUP, prep_group, 0)

    def mix_group(i, carry):
        ns = [i * HGRN_GROUP + j for j in range(HGRN_GROUP)]
        local = []
        for n in ns:
            rows = chunk_rows(n)
            v = i_ref[0, rows, :]
            per_head = []
            for sl in heads():
                a = jnp.where(causal, _dot_nt(qd_ref[rows, sl], kd_ref[rows, sl]), 0.0)
                per_head.append((a.astype(BF16), _dot_tn(v[:, sl], kst_ref[rows, sl])))
            local.append((rows, v, per_head))
        for n, (rows, v, per_head) in zip(ns, local):
            d_last = dl_ref[pl.ds(pl.multiple_of(n * 8, 8), 8), :]
            for h, (sl, (a, upd)) in enumerate(zip(heads(), per_head)):
                st = st_ref[h]
                of_ref[rows, sl] = (jnp.dot(a, v[:, sl], preferred_element_type=F32)
                                    + _dot_nt(qin_ref[rows, sl], st.astype(BF16)))
                st_new = st.reshape(HGRN_DV // 8, 8, HGRN_DK) * d_last[None, :, sl]
                st_ref[h] = st_new.reshape(HGRN_DV, HGRN_DK) + upd
        return carry

    lax.fori_loop(0, n_chunks // HGRN_GROUP, mix_group, 0)

    def norm_block(n, carry):
        rows = chunk_rows(n)
        hg = g_ref[0, rows, :].astype(F32)
        gate = hg + hg * jnp.tanh(hg)
        for sl in heads():
            o = of_ref[rows, sl]
            ms = jnp.mean(o * o, axis=-1, keepdims=True)
            o_ref[0, rows, sl] = (o * lax.rsqrt(ms + RMS_EPS) * nw * gate[:, sl]).astype(o_ref.dtype)
        return carry

    lax.fori_loop(0, n_chunks, norm_block, 0, unroll=2)


def _hgrn(proj3, lb, norm_w):
    B, S, _ = proj3.shape
    ts = HGRN_TILE
    W = HGRN_WIDTH
    spec = lambda j: pl.BlockSpec((1, ts, W), lambda b, s, j=j: (b, s, j))
    vmem = 5 * 2 * ts * W * 2 + HGRN_HEADS * HGRN_DK * HGRN_DV * 4 + (16 << 20)
    return pl.pallas_call(
        _hgrn_kernel,
        grid=(B, S // ts),
        in_specs=[spec(0), spec(1), spec(2), spec(3),
                  pl.BlockSpec((1, W), lambda b, s: (0, 0)),
                  pl.BlockSpec((1, HGRN_DV), lambda b, s: (0, 0))],
        out_specs=pl.BlockSpec((1, ts, W), lambda b, s: (b, s, 0)),
        out_shape=jax.ShapeDtypeStruct((B, S, W), BF16),
        scratch_shapes=[pltpu.VMEM((HGRN_HEADS, HGRN_DV, HGRN_DK), F32),
                        pltpu.VMEM((ts, W), BF16), pltpu.VMEM((ts, W), BF16),
                        pltpu.VMEM((ts, W), BF16), pltpu.VMEM((ts, W), BF16),
                        pltpu.VMEM((ts // CHUNK * 8, W), F32),
                        pltpu.VMEM((ts, W), F32)],
        compiler_params=pltpu.CompilerParams(
            dimension_semantics=("arbitrary", "arbitrary"), vmem_limit_bytes=_vmem_limit(vmem)),
        name="hgrn",
    )(proj3, proj3, proj3, proj3, lb, norm_w)


def _attn_schedule(nq):
    pairs = [(qi, kj) for qi in range(1, nq) for kj in range(qi)]
    nxt = pairs[1:] + pairs[-1:]
    return np.array([[p[0] for p in nxt], [p[1] for p in nxt],
                     [p[0] for p in pairs], [p[1] for p in pairs]], np.int32)


def _attn_kernel(sched_ref, q_ref, k_ref, v_ref, diag_ref, near_ref, lam_ref, nw_ref, o_ref,
                 s0_ref, s1_ref, m_ref, acc_ref, *, block, out_scale):
    tq = block
    half = tq // 2
    nq = q_ref.shape[1] // tq
    n_off = sched_ref.shape[1]
    ones = jnp.ones((block, LANES), BF16)
    lam = lam_ref[...]

    def blk(j):
        return pl.ds(pl.multiple_of(j * block, block), block)

    def stacked_queries(qi):
        q12 = q_ref[0, blk(qi), :]
        lane = lax.broadcasted_iota(jnp.int32, q12.shape, 1)
        zero = jnp.zeros_like(q12)
        q1 = jnp.where(lane < DIFF_DH, q12, zero)
        q2 = jnp.where(lane >= DIFF_DH, q12, zero)
        return jnp.concatenate([q1[:half], q2[:half], q1[half:], q2[half:]], axis=0)

    def values(kj):
        return jnp.concatenate([v_ref[0, blk(kj), :], ones], axis=1)

    def online_softmax(s, rows, qi, v_ext):
        n_lane_tiles = s.shape[1] // LANES
        mx = s[:, :LANES]
        for c in range(1, n_lane_tiles):
            mx = jnp.maximum(mx, s[:, c * LANES:(c + 1) * LANES])
        m_prev = m_ref[qi, rows, :]
        m_new = jnp.maximum(m_prev, jnp.max(mx, axis=-1, keepdims=True))
        alpha = jnp.exp2(m_prev - m_new)
        p = jnp.exp2(s - jnp.concatenate([m_new] * n_lane_tiles, axis=1)).astype(BF16)
        pv = jnp.dot(p, v_ext, preferred_element_type=F32)
        acc_ref[qi, rows, :] = jnp.concatenate([alpha, alpha], axis=1) * acc_ref[qi, rows, :] + pv
        m_ref[qi, rows, :] = m_new

    @pl.when((pl.program_id(0) == 0) & (pl.program_id(1) == 0))
    def _():
        m_ref[...] = jnp.full_like(m_ref, MASK_VALUE)
        acc_ref[...] = jnp.zeros_like(acc_ref)

    def off_scores(qi, kj, s_ref):
        s_ref[...] = _dot_nt(stacked_queries(qi), k_ref[0, blk(kj), :])

    def off_update(step, s_ref):
        qi = sched_ref[2, step]
        kj = sched_ref[3, step]
        corner = jnp.where(qi - kj == 1, 1.0, 0.0) * near_ref[0, 0]
        for r0 in (0, half):
            s_ref[r0:r0 + LANES, tq - LANES:] += corner
        online_softmax(s_ref[...], slice(0, 2 * tq), qi, values(kj))

    def off_pair(i, carry):
        off_scores(sched_ref[0, 2 * i], sched_ref[1, 2 * i], s1_ref)
        off_update(2 * i, s0_ref)
        off_scores(sched_ref[0, 2 * i + 1], sched_ref[1, 2 * i + 1], s0_ref)
        off_update(2 * i + 1, s1_ref)
        return carry

    off_scores(1, 0, s0_ref)
    lax.fori_loop(0, n_off // 2, off_pair, 0)

    def diag_scores(qi, s_ref):
        qz = stacked_queries(qi)
        s_ref[:tq, :half] = _dot_nt(qz[:tq], k_ref[0, pl.ds(pl.multiple_of(qi * tq, tq), half), :])
        s_ref[tq:, :] = _dot_nt(qz[tq:], k_ref[0, blk(qi), :])

    def diag_update(qi, s_ref):
        bias = diag_ref[0, 0]
        v_ext = values(qi)
        bias_a = bias[:half, :half]
        bias_b = bias[half:, :]
        online_softmax(s_ref[:tq, :half] + jnp.concatenate([bias_a, bias_a], axis=0),
                       slice(0, tq), qi, v_ext[:half])
        online_softmax(s_ref[tq:, :] + jnp.concatenate([bias_b, bias_b], axis=0),
                       slice(tq, 2 * tq), qi, v_ext)
        acc = acc_ref[qi]
        o_all = acc[:, :DIFF_DV] / acc[:, DIFF_DV:]
        o = jnp.concatenate([o_all[:half] - lam * o_all[half:tq],
                             o_all[tq:tq + half] - lam * o_all[tq + half:]], axis=0)
        ms = jnp.mean(o * o, axis=-1, keepdims=True)
        y = o * lax.rsqrt(ms + RMS_EPS) * nw_ref[...] * out_scale
        o_ref[0, blk(qi), :] = y.astype(o_ref.dtype)
        m_ref[qi] = jnp.full((2 * tq, LANES), MASK_VALUE, F32)
        acc_ref[qi] = jnp.zeros_like(acc)

    def diag_pair(i, carry):
        diag_scores(jnp.minimum(2 * i + 1, nq - 1), s1_ref)
        diag_update(2 * i, s0_ref)
        diag_scores(jnp.minimum(2 * i + 2, nq - 1), s0_ref)
        diag_update(2 * i + 1, s1_ref)
        return carry

    diag_scores(0, s0_ref)
    lax.fori_loop(0, nq // 2, diag_pair, 0)


def _attn(proj3, bias_tiles, lam, norm_w, *, out_scale):
    B, S, _ = proj3.shape
    H = DIFF_HEADS
    block = ATT_BLOCK
    nq = S // block
    sched = _attn_schedule(nq)
    assert nq % 2 == 0 and sched.shape[1] % 2 == 0, "pipeline steps are unrolled in pairs"
    assert bias_tiles.shape[1] == 2 and _bias_reach() <= LANES <= block // 2
    q_blk = 4 * HGRN_HEADS
    k_blk = q_blk + H
    v_blk = k_blk + H
    seq_spec = lambda j: pl.BlockSpec((1, S, LANES), lambda b, h, j=j: (b, 0, j + h))
    rows = 2 * block
    scratch = [pltpu.VMEM((rows, block), F32), pltpu.VMEM((rows, block), F32),
               pltpu.VMEM((nq, rows, LANES), F32),
               pltpu.VMEM((nq, rows, 2 * LANES), F32)]
    scratch_bytes = rows * (2 * block * 4 + nq * 3 * LANES * 4)
    vmem = (4 * 2 * S * LANES * 2 + 2 * block * block * 4 + scratch_bytes
            + 5 * rows * block * 4 + (6 << 20))
    return pl.pallas_call(
        functools.partial(_attn_kernel, block=block, out_scale=out_scale),
        grid=(B, H),
        in_specs=[
            pl.BlockSpec(memory_space=pltpu.SMEM),
            seq_spec(q_blk), seq_spec(k_blk), seq_spec(v_blk),
            pl.BlockSpec((1, 1, block, block), lambda b, h: (h, 0, 0, 0)),
            pl.BlockSpec((1, 1, LANES, LANES), lambda b, h: (h, 1, 0, block // LANES - 1)),
            pl.BlockSpec((1, LANES), lambda b, h: (0, 0)),
            pl.BlockSpec((1, DIFF_DV), lambda b, h: (0, 0)),
        ],
        out_specs=pl.BlockSpec((1, S, LANES), lambda b, h: (b, 0, h)),
        out_shape=jax.ShapeDtypeStruct((B, S, DIFF_WIDTH), BF16),
        scratch_shapes=scratch,
        compiler_params=pltpu.CompilerParams(
            dimension_semantics=("arbitrary", "arbitrary"),
            vmem_limit_bytes=_vmem_limit(vmem)),
        name="attn",
    )(jnp.asarray(sched), proj3, proj3, proj3, bias_tiles, bias_tiles, lam, norm_w)


def _out_ffn_kernel(x_ref, oh_ref, od_ref, ada_ref, wo_ref, g1_ref, b1_ref,
                    wg_ref, wu_ref, wd_ref, g2_ref, b2_ref, o_ref, x1_ref, u_ref, *, alpha):
    gate_m = ada_ref[0, 2:3, :]
    shift_f = ada_ref[0, 3:4, :]
    scale_f = ada_ref[0, 4:5, :]
    gate_f = ada_ref[0, 5:6, :]
    hw = oh_ref.shape[1]
    sub = x_ref.shape[0] // FFN_SUBTILES
    n_chunks = wg_ref.shape[1] // FF_CHUNK

    def rows_of(t):
        return slice(t * sub, (t + 1) * sub)

    def prep(rows):
        mix = (jnp.dot(oh_ref[rows, :], wo_ref[:hw, :], preferred_element_type=F32)
               + jnp.dot(od_ref[rows, :], wo_ref[hw:, :], preferred_element_type=F32))
        x1 = (_layer_norm_rows(alpha * x_ref[rows, :] + (1.0 + gate_m) * mix) * g1_ref[...]
              + b1_ref[...])
        x1_ref[rows, :] = x1
        u_ref[rows, :] = (_layer_norm_rows(x1) * (1.0 + scale_f) + shift_f).astype(BF16)

    def swiglu(rows):
        u = u_ref[rows, :]
        y = jnp.zeros((sub, o_ref.shape[1]), F32)
        for c in range(n_chunks):
            cols = slice(c * FF_CHUNK, (c + 1) * FF_CHUNK)
            a = jnp.dot(u, wg_ref[:, cols], preferred_element_type=F32)
            b = jnp.dot(u, wu_ref[:, cols], preferred_element_type=F32)
            hid = (a * (1.0 / (1.0 + jnp.exp(-a))) * b).astype(BF16)
            y = y + jnp.dot(hid, wd_ref[cols, :], preferred_element_type=F32)
        return y

    def finish(rows, y):
        o_ref[rows, :] = (_layer_norm_rows(alpha * x1_ref[rows, :] + (1.0 + gate_f) * y)
                          * g2_ref[...] + b2_ref[...])

    prep(rows_of(0))
    for t in range(FFN_SUBTILES):
        if t + 1 < FFN_SUBTILES:
            prep(rows_of(t + 1))
        y = swiglu(rows_of(t))
        finish(rows_of(t), y)


def _out_ffn(x2, oh2, od2, ada3, w_out, ln1_g, ln1_b, w_gate, w_up, w_down, ln2_g, ln2_b,
             *, seq, alpha):
    N, D = x2.shape
    tm = ROW_TILE * FFN_SUBTILES
    sub = ROW_TILE
    per_batch = seq // tm
    row = lambda w: pl.BlockSpec((tm, w), lambda i: (i, 0))
    const = lambda shape: pl.BlockSpec(shape, lambda i: (0, 0), pipeline_mode=pl.Buffered(1))
    vec = pl.BlockSpec((1, D), lambda i: (0, 0))
    weights = (w_out.size + w_gate.size + w_up.size + w_down.size) * 2
    vmem = (weights + 2 * 2 * tm * D * 4 + 2 * 2 * tm * oh2.shape[1] * 2 + tm * D * (4 + 2)
            + 4 * sub * D * 4 + 4 * sub * FF_CHUNK * 4 + (4 << 20))
    return pl.pallas_call(
        functools.partial(_out_ffn_kernel, alpha=alpha),
        grid=(N // tm,),
        in_specs=[
            row(D), row(oh2.shape[1]), row(od2.shape[1]),
            pl.BlockSpec((1, ada3.shape[1], D), lambda i: (i // per_batch, 0, 0)),
            const(w_out.shape), vec, vec,
            const(w_gate.shape), const(w_up.shape), const(w_down.shape), vec, vec,
        ],
        out_specs=row(D),
        out_shape=jax.ShapeDtypeStruct((N, D), F32),
        scratch_shapes=[pltpu.VMEM((tm, D), F32), pltpu.VMEM((tm, D), BF16)],
        compiler_params=pltpu.CompilerParams(
            dimension_semantics=("arbitrary",), vmem_limit_bytes=_vmem_limit(vmem)),
        name="out_ffn",
    )(x2, oh2, od2, ada3, w_out, ln1_g, ln1_b, w_gate, w_up, w_down, ln2_g, ln2_b)


def kernel(x, c, w_ada, b_ada, w_in, lb_logits, hgrn_norm_w, lam_q1, lam_k1, lam_q2, lam_k2,
           diff_norm_w, rel_bias, w_out, ln1_g, ln1_b, w_gate, w_up, w_down, ln2_g, ln2_b):
    B, S, D = x.shape
    depth = w_ada.shape[0]
    assert S % (ROW_TILE * max(FFN_SUBTILES, PROJ_SUBTILES)) == 0
    assert S % HGRN_TILE == 0 and S % ATT_BLOCK == 0
    assert HGRN_TILE % (CHUNK * HGRN_GROUP) == 0 and w_in.shape[2] % PROJ_CHUNK == 0
    assert w_gate.shape[2] % FF_CHUNK == 0
    alpha = (2.0 * depth) ** 0.25
    bias_tiles = _bias_tiles(rel_bias, block=ATT_BLOCK, n_tiles=_num_near_tiles(ATT_BLOCK))
    col_scales = ((0, HGRN_WIDTH, HGRN_DK ** -0.5),
                  (HGRN_WIDTH, 2 * HGRN_WIDTH, 0.5),
                  (3 * HGRN_WIDTH, 4 * HGRN_WIDTH, 0.5),
                  (4 * HGRN_WIDTH, 4 * HGRN_WIDTH + DIFF_WIDTH, LOG2E * DIFF_DH ** -0.5))
    for l in range(depth):
        lam_init = 0.8 - 0.6 * math.exp(-0.3 * l)
        ada, lb, lam = _prep(c, w_ada[l], b_ada[l][None, :], lb_logits,
                             lam_q1[l][None, :], lam_k1[l][None, :],
                             lam_q2[l][None, :], lam_k2[l][None, :],
                             layer=l, lam_init=lam_init)
        ada3 = ada.reshape(B, 6, D)
        x2 = x.reshape(B * S, D)
        proj = _in_proj(x2, ada3, w_in[l].astype(BF16), seq=S, col_scales=col_scales)
        proj3 = proj.reshape(B, S, proj.shape[1])
        o_h = _hgrn(proj3, lb, hgrn_norm_w[l][None, :])
        o_d = _attn(proj3, bias_tiles, lam, diff_norm_w[l][None, :],
                    out_scale=1.0 - lam_init)
        out = _out_ffn(x2, o_h.reshape(B * S, -1), o_d.reshape(B * S, -1), ada3,
                       w_out[l].astype(BF16), ln1_g[l][None, :], ln1_b[l][None, :],
                       w_gate[l].astype(BF16), w_up[l].astype(BF16), w_down[l].astype(BF16),
                       ln2_g[l][None, :], ln2_b[l][None, :], seq=S, alpha=alpha)
        x = out.reshape(B, S, D)
    return x
```

```python
import functools
import math

import jax
import jax.numpy as jnp
import numpy as np
from jax import lax
from jax.experimental import pallas as pl
from jax.experimental.pallas import tpu as pltpu

HGRN_HEADS = 4
HGRN_DK = 128
HGRN_DV = 128
HGRN_WIDTH = HGRN_HEADS * HGRN_DV
DIFF_HEADS = 4
DIFF_DH = 64
DIFF_DV = 2 * DIFF_DH
DIFF_WIDTH = DIFF_HEADS * DIFF_DV
N_BUCKETS = 32
MAX_DISTANCE = 128
CHUNK = 64
LN_EPS = 1e-5
RMS_EPS = 1e-6
LOG2E = math.log2(math.e)

LANES = 128
V7X_VMEM_BYTES = 64 * 1024 * 1024

ROW_TILE = 512
HGRN_TILE = 512
HGRN_GROUP = 4
ATT_BLOCK = 512
ATT_UNROLL = 14
FF_CHUNK = 256
FFN_SUBTILES = 2
PROJ_SUBTILES = 2
PROJ_CHUNK = 512
MASK_VALUE = -1e30

BF16 = jnp.bfloat16
F32 = jnp.float32


def _vmem_limit(block_bytes):
    return int(min(block_bytes, V7X_VMEM_BYTES - 4 * 1024 * 1024))


def _layer_norm_rows(x):
    mu = jnp.mean(x, axis=-1, keepdims=True)
    xc = x - mu
    var = jnp.mean(xc * xc, axis=-1, keepdims=True)
    return xc * lax.rsqrt(var + LN_EPS)


def _dot_nt(a, b):
    return lax.dot_general(a, b, (((1,), (1,)), ((), ())), preferred_element_type=F32)


def _dot_tn(a, b):
    return lax.dot_general(a, b, (((0,), (0,)), ((), ())), preferred_element_type=F32)


def _prep_kernel(c_ref, w_ref, b_ref, lbl_ref, q1_ref, k1_ref, q2_ref, k2_ref,
                 ada_ref, lb_ref, lam_ref, *, layer, lam_init):
    c = c_ref[...]
    sc = c * (1.0 / (1.0 + jnp.exp(-c)))
    ada_ref[...] = jnp.dot(sc.astype(BF16), w_ref[...].astype(BF16),
                           preferred_element_type=F32) + b_ref[...]

    @pl.when(pl.program_id(0) == 0)
    def _():
        n_rows = lbl_ref.shape[0]
        rows = [lbl_ref[r:r + 1, :] for r in range(n_rows)]
        mx = functools.reduce(jnp.maximum, rows)
        es = [jnp.exp(r - mx) for r in rows]
        den = functools.reduce(lambda a, b: a + b, es)
        num = functools.reduce(lambda a, b: a + b, es[:layer + 1])
        lb_ref[...] = num / den
        s1 = jnp.sum(q1_ref[...] * k1_ref[...], axis=-1, keepdims=True)
        s2 = jnp.sum(q2_ref[...] * k2_ref[...], axis=-1, keepdims=True)
        lam = jnp.exp(s1) - jnp.exp(s2) + lam_init
        lam_ref[...] = jnp.broadcast_to(lam, lam_ref.shape)


def _prep(c, w_ada, b_ada, lb_logits, lam_q1, lam_k1, lam_q2, lam_k2, *, layer, lam_init):
    B, D = c.shape
    n_out = w_ada.shape[1]
    col = D
    width = lb_logits.shape[1]
    small = lambda shape: pl.BlockSpec(shape, lambda j: (0,) * len(shape))
    return pl.pallas_call(
        functools.partial(_prep_kernel, layer=layer, lam_init=lam_init),
        grid=(n_out // col,),
        in_specs=[
            small((B, D)),
            pl.BlockSpec((D, col), lambda j: (0, j)),
            pl.BlockSpec((1, col), lambda j: (0, j)),
            small(lb_logits.shape),
            small((1, DIFF_DH)), small((1, DIFF_DH)), small((1, DIFF_DH)), small((1, DIFF_DH)),
        ],
        out_specs=[
            pl.BlockSpec((B, col), lambda j: (0, j)),
            small((1, width)),
            small((1, LANES)),
        ],
        out_shape=[
            jax.ShapeDtypeStruct((B, n_out), F32),
            jax.ShapeDtypeStruct((1, width), F32),
            jax.ShapeDtypeStruct((1, LANES), F32),
        ],
        compiler_params=pltpu.CompilerParams(
            dimension_semantics=("arbitrary",),
            vmem_limit_bytes=_vmem_limit(4 * D * col * 4 + (8 << 20))),
        name="prep",
    )(c, w_ada, b_ada, lb_logits, lam_q1, lam_k1, lam_q2, lam_k2)


def _t5_bucket_np(dist):
    max_exact = N_BUCKETS // 2
    d = np.maximum(dist, 1).astype(np.float32)
    large = max_exact + (np.log(d / max_exact) / math.log(MAX_DISTANCE / max_exact)
                         * (N_BUCKETS - max_exact)).astype(np.int32)
    large = np.minimum(large, N_BUCKETS - 1)
    return np.where(dist < max_exact, dist, large)


def _bias_reach():
    first_last = int(np.argmax(_t5_bucket_np(np.arange(0, 4 * MAX_DISTANCE)) == N_BUCKETS - 1))
    assert np.all(_t5_bucket_np(np.arange(first_last, 64 * MAX_DISTANCE)) == N_BUCKETS - 1)
    return first_last


def _num_near_tiles(block):
    t = 0
    while t * block - (block - 1) < _bias_reach():
        t += 1
    return t


def _bias_kernel(tab_ref, o_ref, *, block):
    h = pl.program_id(0)
    t = pl.program_id(1)
    r = lax.broadcasted_iota(jnp.int32, (block, block), 0)
    c = lax.broadcasted_iota(jnp.int32, (block, block), 1)
    dist = r - c + t * block
    rel = jnp.maximum(dist, 0)
    max_exact = N_BUCKETS // 2
    d = jnp.maximum(rel, 1).astype(F32)
    large = max_exact + (jnp.log(d / max_exact) / math.log(MAX_DISTANCE / max_exact)
                         * (N_BUCKETS - max_exact)).astype(jnp.int32)
    large = jnp.minimum(large, N_BUCKETS - 1)
    bucket = jnp.where(rel < max_exact, rel, large)
    bias = jnp.zeros((block, block), F32)
    for b in range(N_BUCKETS):
        bias = jnp.where(bucket == b, tab_ref[b, h], bias)
    shifted = (bias - tab_ref[N_BUCKETS - 1, h]) * LOG2E
    o_ref[0, 0] = jnp.where(dist >= 0, shifted, MASK_VALUE)


def _bias_tiles(rel_bias, *, block, n_tiles):
    H = rel_bias.shape[1]
    return pl.pallas_call(
        functools.partial(_bias_kernel, block=block),
        grid=(H, n_tiles),
        in_specs=[pl.BlockSpec(memory_space=pltpu.SMEM)],
        out_specs=pl.BlockSpec((1, 1, block, block), lambda h, t: (h, t, 0, 0)),
        out_shape=jax.ShapeDtypeStruct((H, n_tiles, block, block), F32),
        compiler_params=pltpu.CompilerParams(dimension_semantics=("arbitrary", "arbitrary")),
        name="bias_tiles",
    )(rel_bias)


def _in_proj_kernel(x_ref, ada_ref, w_ref, o_ref, *, col_scales):
    shift = ada_ref[0, 0:1, :]
    scale = ada_ref[0, 1:2, :]
    n_out = o_ref.shape[1]
    sub = x_ref.shape[0] // PROJ_SUBTILES

    def modulated(t):
        x = x_ref[t * sub:(t + 1) * sub, :]
        return (_layer_norm_rows(x) * (1.0 + scale) + shift).astype(BF16)

    u = modulated(0)
    for t in range(PROJ_SUBTILES):
        u_next = modulated(t + 1) if t + 1 < PROJ_SUBTILES else None
        for c0 in range(0, n_out, PROJ_CHUNK):
            acc = jnp.dot(u, w_ref[:, c0:c0 + PROJ_CHUNK], preferred_element_type=F32)
            for lo, hi, s in col_scales:
                if lo <= c0 < hi:
                    acc = acc * s
            o_ref[t * sub:(t + 1) * sub, c0:c0 + PROJ_CHUNK] = acc.astype(o_ref.dtype)
        u = u_next


def _in_proj(x2, ada3, w_in, *, seq, col_scales):
    N, D = x2.shape
    n_out = w_in.shape[1]
    tm = ROW_TILE * PROJ_SUBTILES
    per_batch = seq // tm
    assert all(lo % PROJ_CHUNK == 0 and hi % PROJ_CHUNK == 0 for lo, hi, _ in col_scales)
    vmem = (2 * tm * D * 4 + D * n_out * 2 + 2 * tm * n_out * 2 + 2 * 8 * D * 4
            + 3 * tm * D * 4 + 2 * tm * PROJ_CHUNK * 4 + (4 << 20))
    return pl.pallas_call(
        functools.partial(_in_proj_kernel, col_scales=col_scales),
        grid=(N // tm,),
        in_specs=[
            pl.BlockSpec((tm, D), lambda i: (i, 0)),
            pl.BlockSpec((1, ada3.shape[1], D), lambda i: (i // per_batch, 0, 0)),
            pl.BlockSpec((D, n_out), lambda i: (0, 0), pipeline_mode=pl.Buffered(1)),
        ],
        out_specs=pl.BlockSpec((tm, n_out), lambda i: (i, 0)),
        out_shape=jax.ShapeDtypeStruct((N, n_out), BF16),
        compiler_params=pltpu.CompilerParams(
            dimension_semantics=("arbitrary",), vmem_limit_bytes=_vmem_limit(vmem)),
        name="in_proj",
    )(x2, ada3, w_in)


def _decay_matrix():
    n = 2 * CHUNK + 16
    r = lax.broadcasted_iota(jnp.int32, (n, CHUNK), 0)
    s = lax.broadcasted_iota(jnp.int32, (n, CHUNK), 1)
    upto_mid = (s < CHUNK // 2).astype(F32)
    d1 = (s <= r).astype(F32) - upto_mid
    d2 = (s > r - CHUNK).astype(F32)
    return jnp.where(r < CHUNK, d1, jnp.where(r < 2 * CHUNK, d2,
                     jnp.where(r < 2 * CHUNK + 8, upto_mid, 1.0))).astype(BF16)


def _split3(x):
    hi = x.astype(BF16)
    r1 = x - hi.astype(F32)
    mid = r1.astype(BF16)
    lo = (r1 - mid.astype(F32)).astype(BF16)
    return hi, mid, lo


def _hgrn_kernel(q_ref, f_ref, i_ref, g_ref, lb_ref, nw_ref, o_ref,
                 st_ref, qd_ref, kd_ref, qin_ref, kst_ref, dl_ref, of_ref):
    @pl.when(pl.program_id(1) == 0)
    def _():
        st_ref[...] = jnp.zeros_like(st_ref)

    lb = lb_ref[...]
    one_m_lb = 1.0 - lb
    nw = nw_ref[...]
    n_chunks = q_ref.shape[1] // CHUNK
    tri_r = lax.broadcasted_iota(jnp.int32, (CHUNK, CHUNK), 0)
    tri_c = lax.broadcasted_iota(jnp.int32, (CHUNK, CHUNK), 1)
    causal = tri_r >= tri_c
    decay = _decay_matrix()
    decay3 = jnp.concatenate([decay, decay, decay], axis=1)
    sub = CHUNK // 8

    half_span = 0.5 * one_m_lb
    f_mid = lb + half_span

    def chunk_rows(n):
        return pl.ds(pl.multiple_of(n * CHUNK, CHUNK), CHUNK)

    def heads():
        return [slice(h * HGRN_DK, (h + 1) * HGRN_DK) for h in range(HGRN_HEADS)]


    def gates(n):
        x = f_ref[0, chunk_rows(n), :].astype(F32)
        th = half_span * jnp.tanh(x)
        lf2 = jnp.log2(f_mid + th)
        return half_span - th, jnp.concatenate(_split3(lf2), axis=0)

    def operands(n, kc, rel):
        rows = chunk_rows(n)
        q = q_ref[0, rows, :].astype(F32)
        d_mid = rel[:CHUNK]
        e1 = jnp.exp2(d_mid)
        qd_f = q * e1
        qd_ref[rows, :] = qd_f.astype(BF16)
        kd_ref[rows, :] = (kc * jnp.exp2(-d_mid)).astype(BF16)
        e_mid = jnp.exp2(rel[2 * CHUNK:2 * CHUNK + 8])
        qin_ref[rows, :] = (qd_f.reshape(sub, 8, -1) * e_mid[None]).reshape(qd_f.shape).astype(BF16)
        kst_ref[rows, :] = (kc * jnp.exp2(rel[CHUNK:2 * CHUNK])).astype(BF16)
        dl_ref[pl.ds(pl.multiple_of(n * 8, 8), 8), :] = jnp.exp2(rel[2 * CHUNK + 8:])

    def prep_group(i, carry):
        ns = [i * HGRN_GROUP + j for j in range(HGRN_GROUP)]
        g1 = [gates(n) for n in ns]
        rels = [jnp.dot(decay3, parts, preferred_element_type=F32) for _, parts in g1]
        for n, (kc, _), rel in zip(ns, g1, rels):
            operands(n, kc, rel)
        return carry

    lax.fori_loop(0, n_chunks // HGRN_GROUP, prep_group, 0)

    def mix_group(i, carry):
        ns = [i * HGRN_GROUP + j for j in range(HGRN_GROUP)]
        local = []
        for n in ns:
            rows = chunk_rows(n)
            v = i_ref[0, rows, :]
            per_head = []
            for sl in heads():
                a = jnp.where(causal, _dot_nt(qd_ref[rows, sl], kd_ref[rows, sl]), 0.0)
                per_head.append((a.astype(BF16), _dot_tn(v[:, sl], kst_ref[rows, sl])))
            local.append((rows, v, per_head))
        for n, (rows, v, per_head) in zip(ns, local):
            d_last = dl_ref[pl.ds(pl.multiple_of(n * 8, 8), 8), :]
            for h, (sl, (a, upd)) in enumerate(zip(heads(), per_head)):
                st = st_ref[h]
                of_ref[rows, sl] = (jnp.dot(a, v[:, sl], preferred_element_type=F32)
                                    + _dot_nt(qin_ref[rows, sl], st.astype(BF16)))
                st_new = st.reshape(HGRN_DV // 8, 8, HGRN_DK) * d_last[None, :, sl]
                st_ref[h] = st_new.reshape(HGRN_DV, HGRN_DK) + upd
        return carry

    lax.fori_loop(0, n_chunks // HGRN_GROUP, mix_group, 0)

    def norm_block(n, carry):
        rows = chunk_rows(n)
        hg = g_ref[0, rows, :].astype(F32)
        gate = hg + hg * jnp.tanh(hg)
        for sl in heads():
            o = of_ref[rows, sl]
            ms = jnp.mean(o * o, axis=-1, keepdims=True)
            o_ref[0, rows, sl] = (o * lax.rsqrt(ms + RMS_EPS) * nw * gate[:, sl]).astype(o_ref.dtype)
        return carry

    lax.fori_loop(0, n_chunks, norm_block, 0, unroll=2)


def _hgrn(proj3, lb, norm_w):
    B, S, _ = proj3.shape
    ts = HGRN_TILE
    W = HGRN_WIDTH
    spec = lambda j: pl.BlockSpec((1, ts, W), lambda b, s, j=j: (b, s, j))
    vmem = 5 * 2 * ts * W * 2 + HGRN_HEADS * HGRN_DK * HGRN_DV * 4 + (16 << 20)
    return pl.pallas_call(
        _hgrn_kernel,
        grid=(B, S // ts),
        in_specs=[spec(0), spec(1), spec(2), spec(3),
                  pl.BlockSpec((1, W), lambda b, s: (0, 0)),
                  pl.BlockSpec((1, HGRN_DV), lambda b, s: (0, 0))],
        out_specs=pl.BlockSpec((1, ts, W), lambda b, s: (b, s, 0)),
        out_shape=jax.ShapeDtypeStruct((B, S, W), BF16),
        scratch_shapes=[pltpu.VMEM((HGRN_HEADS, HGRN_DV, HGRN_DK), F32),
                        pltpu.VMEM((ts, W), BF16), pltpu.VMEM((ts, W), BF16),
                        pltpu.VMEM((ts, W), BF16), pltpu.VMEM((ts, W), BF16),
                        pltpu.VMEM((ts // CHUNK * 8, W), F32),
                        pltpu.VMEM((ts, W), F32)],
        compiler_params=pltpu.CompilerParams(
            dimension_semantics=("arbitrary", "arbitrary"), vmem_limit_bytes=_vmem_limit(vmem)),
        name="hgrn",
    )(proj3, proj3, proj3, proj3, lb, norm_w)


def _attn_schedule(nq):
    pairs = [(qi, kj) for qi in range(1, nq) for kj in range(qi)]
    nxt = pairs[1:] + pairs[-1:]
    return np.array([[p[0] for p in nxt], [p[1] for p in nxt],
                     [p[0] for p in pairs], [p[1] for p in pairs]], np.int32)


def _attn_kernel(sched_ref, q_ref, k_ref, v_ref, diag_ref, near_ref, lam_ref, nw_ref, o_ref,
                 s0_ref, s1_ref, m_ref, acc_ref, *, block, out_scale):
    tq = block
    half = tq // 2
    nq = q_ref.shape[1] // tq
    n_off = sched_ref.shape[1]
    ones = jnp.ones((block, LANES), BF16)
    lam = lam_ref[...]

    def blk(j):
        return pl.ds(pl.multiple_of(j * block, block), block)

    def stacked_queries(qi):
        q12 = q_ref[0, blk(qi), :]
        lane = lax.broadcasted_iota(jnp.int32, q12.shape, 1)
        zero = jnp.zeros_like(q12)
        q1 = jnp.where(lane < DIFF_DH, q12, zero)
        q2 = jnp.where(lane >= DIFF_DH, q12, zero)
        return jnp.concatenate([q1[:half], q2[:half], q1[half:], q2[half:]], axis=0)

    def values(kj):
        return jnp.concatenate([v_ref[0, blk(kj), :], ones], axis=1)

    def online_softmax(s, rows, qi, v_ext):
        n_lane_tiles = s.shape[1] // LANES
        mx = s[:, :LANES]
        for c in range(1, n_lane_tiles):
            mx = jnp.maximum(mx, s[:, c * LANES:(c + 1) * LANES])
        m_prev = m_ref[qi, rows, :]
        m_new = jnp.maximum(m_prev, jnp.max(mx, axis=-1, keepdims=True))
        alpha = jnp.exp2(m_prev - m_new)
        p = jnp.exp2(s - jnp.concatenate([m_new] * n_lane_tiles, axis=1)).astype(BF16)
        pv = jnp.dot(p, v_ext, preferred_element_type=F32)
        acc_ref[qi, rows, :] = jnp.concatenate([alpha, alpha], axis=1) * acc_ref[qi, rows, :] + pv
        m_ref[qi, rows, :] = m_new

    @pl.when((pl.program_id(0) == 0) & (pl.program_id(1) == 0))
    def _():
        m_ref[...] = jnp.full_like(m_ref, MASK_VALUE)
        acc_ref[...] = jnp.zeros_like(acc_ref)

    def off_scores(qi, kj, s_ref):
        s_ref[...] = _dot_nt(stacked_queries(qi), k_ref[0, blk(kj), :])

    def off_update(step, s_ref):
        qi = sched_ref[2, step]
        kj = sched_ref[3, step]
        corner = jnp.where(qi - kj == 1, 1.0, 0.0) * near_ref[0, 0]
        for r0 in (0, half):
            s_ref[r0:r0 + LANES, tq - LANES:] += corner
        online_softmax(s_ref[...], slice(0, 2 * tq), qi, values(kj))

    unroll = max(u for u in range(2, ATT_UNROLL + 1, 2) if n_off % u == 0)

    def off_steps(i, carry):
        for j in range(0, unroll, 2):
            step = unroll * i + j
            off_scores(sched_ref[0, step], sched_ref[1, step], s1_ref)
            off_update(step, s0_ref)
            off_scores(sched_ref[0, step + 1], sched_ref[1, step + 1], s0_ref)
            off_update(step + 1, s1_ref)
        return carry

    off_scores(1, 0, s0_ref)
    lax.fori_loop(0, n_off // unroll, off_steps, 0)

    def diag_scores(qi, s_ref):
        qz = stacked_queries(qi)
        s_ref[:tq, :half] = _dot_nt(qz[:tq], k_ref[0, pl.ds(pl.multiple_of(qi * tq, tq), half), :])
        s_ref[tq:, :] = _dot_nt(qz[tq:], k_ref[0, blk(qi), :])

    def diag_update(qi, s_ref):
        bias = diag_ref[0, 0]
        v_ext = values(qi)
        bias_a = bias[:half, :half]
        bias_b = bias[half:, :]
        online_softmax(s_ref[:tq, :half] + jnp.concatenate([bias_a, bias_a], axis=0),
                       slice(0, tq), qi, v_ext[:half])
        online_softmax(s_ref[tq:, :] + jnp.concatenate([bias_b, bias_b], axis=0),
                       slice(tq, 2 * tq), qi, v_ext)
        acc = acc_ref[qi]
        o_all = acc[:, :DIFF_DV] / acc[:, DIFF_DV:]
        o = jnp.concatenate([o_all[:half] - lam * o_all[half:tq],
                             o_all[tq:tq + half] - lam * o_all[tq + half:]], axis=0)
        ms = jnp.mean(o * o, axis=-1, keepdims=True)
        y = o * lax.rsqrt(ms + RMS_EPS) * nw_ref[...] * out_scale
        o_ref[0, blk(qi), :] = y.astype(o_ref.dtype)
        m_ref[qi] = jnp.full((2 * tq, LANES), MASK_VALUE, F32)
        acc_ref[qi] = jnp.zeros_like(acc)

    def diag_pair(i, carry):
        diag_scores(jnp.minimum(2 * i + 1, nq - 1), s1_ref)
        diag_update(2 * i, s0_ref)
        diag_scores(jnp.minimum(2 * i + 2, nq - 1), s0_ref)
        diag_update(2 * i + 1, s1_ref)
        return carry

    diag_scores(0, s0_ref)
    lax.fori_loop(0, nq // 2, diag_pair, 0)


def _attn(proj3, bias_tiles, lam, norm_w, *, out_scale):
    B, S, _ = proj3.shape
    H = DIFF_HEADS
    block = ATT_BLOCK
    nq = S // block
    sched = _attn_schedule(nq)
    assert nq % 2 == 0 and sched.shape[1] % 2 == 0, "pipeline steps are unrolled in pairs"
    assert bias_tiles.shape[1] == 2 and _bias_reach() <= LANES <= block // 2
    q_blk = 4 * HGRN_HEADS
    k_blk = q_blk + H
    v_blk = k_blk + H
    seq_spec = lambda j: pl.BlockSpec((1, S, LANES), lambda b, h, j=j: (b, 0, j + h))
    rows = 2 * block
    scratch = [pltpu.VMEM((rows, block), F32), pltpu.VMEM((rows, block), F32),
               pltpu.VMEM((nq, rows, LANES), F32),
               pltpu.VMEM((nq, rows, 2 * LANES), F32)]
    scratch_bytes = rows * (2 * block * 4 + nq * 3 * LANES * 4)
    vmem = (4 * 2 * S * LANES * 2 + 2 * block * block * 4 + scratch_bytes
            + 5 * rows * block * 4 + (6 << 20))
    return pl.pallas_call(
        functools.partial(_attn_kernel, block=block, out_scale=out_scale),
        grid=(B, H),
        in_specs=[
            pl.BlockSpec(memory_space=pltpu.SMEM),
            seq_spec(q_blk), seq_spec(k_blk), seq_spec(v_blk),
            pl.BlockSpec((1, 1, block, block), lambda b, h: (h, 0, 0, 0)),
            pl.BlockSpec((1, 1, LANES, LANES), lambda b, h: (h, 1, 0, block // LANES - 1)),
            pl.BlockSpec((1, LANES), lambda b, h: (0, 0)),
            pl.BlockSpec((1, DIFF_DV), lambda b, h: (0, 0)),
        ],
        out_specs=pl.BlockSpec((1, S, LANES), lambda b, h: (b, 0, h)),
        out_shape=jax.ShapeDtypeStruct((B, S, DIFF_WIDTH), BF16),
        scratch_shapes=scratch,
        compiler_params=pltpu.CompilerParams(
            dimension_semantics=("arbitrary", "arbitrary"),
            vmem_limit_bytes=_vmem_limit(vmem)),
        name="attn",
    )(jnp.asarray(sched), proj3, proj3, proj3, bias_tiles, bias_tiles, lam, norm_w)


def _out_ffn_kernel(x_ref, oh_ref, od_ref, ada_ref, wo_ref, g1_ref, b1_ref,
                    wg_ref, wu_ref, wd_ref, g2_ref, b2_ref, o_ref, x1_ref, u_ref, *, alpha):
    gate_m = ada_ref[0, 2:3, :]
    shift_f = ada_ref[0, 3:4, :]
    scale_f = ada_ref[0, 4:5, :]
    gate_f = ada_ref[0, 5:6, :]
    hw = oh_ref.shape[1]
    sub = x_ref.shape[0] // FFN_SUBTILES
    n_chunks = wg_ref.shape[1] // FF_CHUNK

    def rows_of(t):
        return slice(t * sub, (t + 1) * sub)

    def prep(rows):
        mix = (jnp.dot(oh_ref[rows, :], wo_ref[:hw, :], preferred_element_type=F32)
               + jnp.dot(od_ref[rows, :], wo_ref[hw:, :], preferred_element_type=F32))
        x1 = (_layer_norm_rows(alpha * x_ref[rows, :] + (1.0 + gate_m) * mix) * g1_ref[...]
              + b1_ref[...])
        x1_ref[rows, :] = x1
        u_ref[rows, :] = (_layer_norm_rows(x1) * (1.0 + scale_f) + shift_f).astype(BF16)

    def swiglu(rows):
        u = u_ref[rows, :]
        y = jnp.zeros((sub, o_ref.shape[1]), F32)
        for c in range(n_chunks):
            cols = slice(c * FF_CHUNK, (c + 1) * FF_CHUNK)
            a = jnp.dot(u, wg_ref[:, cols], preferred_element_type=F32)
            b = jnp.dot(u, wu_ref[:, cols], preferred_element_type=F32)
            hid = (a * (1.0 / (1.0 + jnp.exp(-a))) * b).astype(BF16)
            y = y + jnp.dot(hid, wd_ref[cols, :], preferred_element_type=F32)
        return y

    def finish(rows, y):
        o_ref[rows, :] = (_layer_norm_rows(alpha * x1_ref[rows, :] + (1.0 + gate_f) * y)
                          * g2_ref[...] + b2_ref[...])

    prep(rows_of(0))
    for t in range(FFN_SUBTILES):
        if t + 1 < FFN_SUBTILES:
            prep(rows_of(t + 1))
        y = swiglu(rows_of(t))
        finish(rows_of(t), y)


def _out_ffn(x2, oh2, od2, ada3, w_out, ln1_g, ln1_b, w_gate, w_up, w_down, ln2_g, ln2_b,
             *, seq, alpha):
    N, D = x2.shape
    tm = ROW_TILE * FFN_SUBTILES
    sub = ROW_TILE
    per_batch = seq // tm
    row = lambda w: pl.BlockSpec((tm, w), lambda i: (i, 0))
    const = lambda shape: pl.BlockSpec(shape, lambda i: (0, 0), pipeline_mode=pl.Buffered(1))
    vec = pl.BlockSpec((1, D), lambda i: (0, 0))
    weights = (w_out.size + w_gate.size + w_up.size + w_down.size) * 2
    vmem = (weights + 2 * 2 * tm * D * 4 + 2 * 2 * tm * oh2.shape[1] * 2 + tm * D * (4 + 2)
            + 4 * sub * D * 4 + 4 * sub * FF_CHUNK * 4 + (4 << 20))
    return pl.pallas_call(
        functools.partial(_out_ffn_kernel, alpha=alpha),
        grid=(N // tm,),
        in_specs=[
            row(D), row(oh2.shape[1]), row(od2.shape[1]),
            pl.BlockSpec((1, ada3.shape[1], D), lambda i: (i // per_batch, 0, 0)),
            const(w_out.shape), vec, vec,
            const(w_gate.shape), const(w_up.shape), const(w_down.shape), vec, vec,
        ],
        out_specs=row(D),
        out_shape=jax.ShapeDtypeStruct((N, D), F32),
        scratch_shapes=[pltpu.VMEM((tm, D), F32), pltpu.VMEM((tm, D), BF16)],
        compiler_params=pltpu.CompilerParams(
            dimension_semantics=("arbitrary",), vmem_limit_bytes=_vmem_limit(vmem)),
        name="out_ffn",
    )(x2, oh2, od2, ada3, w_out, ln1_g, ln1_b, w_gate, w_up, w_down, ln2_g, ln2_b)


def kernel(x, c, w_ada, b_ada, w_in, lb_logits, hgrn_norm_w, lam_q1, lam_k1, lam_q2, lam_k2,
           diff_norm_w, rel_bias, w_out, ln1_g, ln1_b, w_gate, w_up, w_down, ln2_g, ln2_b):
    B, S, D = x.shape
    depth = w_ada.shape[0]
    assert S % (ROW_TILE * max(FFN_SUBTILES, PROJ_SUBTILES)) == 0
    assert S % HGRN_TILE == 0 and S % ATT_BLOCK == 0
    assert HGRN_TILE % (CHUNK * HGRN_GROUP) == 0 and w_in.shape[2] % PROJ_CHUNK == 0
    assert w_gate.shape[2] % FF_CHUNK == 0
    alpha = (2.0 * depth) ** 0.25
    bias_tiles = _bias_tiles(rel_bias, block=ATT_BLOCK, n_tiles=_num_near_tiles(ATT_BLOCK))
    col_scales = ((0, HGRN_WIDTH, HGRN_DK ** -0.5),
                  (HGRN_WIDTH, 2 * HGRN_WIDTH, 0.5),
                  (3 * HGRN_WIDTH, 4 * HGRN_WIDTH, 0.5),
                  (4 * HGRN_WIDTH, 4 * HGRN_WIDTH + DIFF_WIDTH, LOG2E * DIFF_DH ** -0.5))
    for l in range(depth):
        lam_init = 0.8 - 0.6 * math.exp(-0.3 * l)
        ada, lb, lam = _prep(c, w_ada[l], b_ada[l][None, :], lb_logits,
                             lam_q1[l][None, :], lam_k1[l][None, :],
                             lam_q2[l][None, :], lam_k2[l][None, :],
                             layer=l, lam_init=lam_init)
        ada3 = ada.reshape(B, 6, D)
        x2 = x.reshape(B * S, D)
        proj = _in_proj(x2, ada3, w_in[l].astype(BF16), seq=S, col_scales=col_scales)
        proj3 = proj.reshape(B, S, proj.shape[1])
        o_h = _hgrn(proj3, lb, hgrn_norm_w[l][None, :])
        o_d = _attn(proj3, bias_tiles, lam, diff_norm_w[l][None, :],
                    out_scale=1.0 - lam_init)
        out = _out_ffn(x2, o_h.reshape(B * S, -1), o_d.reshape(B * S, -1), ada3,
                       w_out[l].astype(BF16), ln1_g[l][None, :], ln1_b[l][None, :],
                       w_gate[l].astype(BF16), w_up[l].astype(BF16), w_down[l].astype(BF16),
                       ln2_g[l][None, :], ln2_b[l][None, :], seq=S, alpha=alpha)
        x = out.reshape(B, S, D)
    return x
```

```python
import functools
import math

import jax
import jax.numpy as jnp
import numpy as np
from jax import lax
from jax.experimental import pallas as pl
from jax.experimental.pallas import tpu as pltpu

HGRN_HEADS = 4
HGRN_DK = 128
HGRN_DV = 128
HGRN_WIDTH = HGRN_HEADS * HGRN_DV
DIFF_HEADS = 4
DIFF_DH = 64
DIFF_DV = 2 * DIFF_DH
DIFF_WIDTH = DIFF_HEADS * DIFF_DV
N_BUCKETS = 32
MAX_DISTANCE = 128
CHUNK = 64
LN_EPS = 1e-5
RMS_EPS = 1e-6
LOG2E = math.log2(math.e)

LANES = 128
V7X_VMEM_BYTES = 64 * 1024 * 1024

ROW_TILE = 512
HGRN_TILE = 1024
HGRN_GROUP = 2
ATT_BLOCK = 512
ATT_UNROLL = 14
FF_CHUNK = 256
FFN_SUBTILES = 2
PROJ_SUBTILES = 2
PROJ_CHUNK = 512
MASK_VALUE = -1e30

BF16 = jnp.bfloat16
F32 = jnp.float32


def _vmem_limit(block_bytes):
    return int(min(block_bytes, V7X_VMEM_BYTES - 4 * 1024 * 1024))


def _layer_norm_rows(x):
    mu = jnp.mean(x, axis=-1, keepdims=True)
    xc = x - mu
    var = jnp.mean(xc * xc, axis=-1, keepdims=True)
    return xc * lax.rsqrt(var + LN_EPS)


def _dot_nt(a, b):
    return lax.dot_general(a, b, (((1,), (1,)), ((), ())), preferred_element_type=F32)


def _dot_tn(a, b):
    return lax.dot_general(a, b, (((0,), (0,)), ((), ())), preferred_element_type=F32)


def _prep_kernel(c_ref, w_ref, b_ref, lbl_ref, q1_ref, k1_ref, q2_ref, k2_ref,
                 ada_ref, lb_ref, lam_ref, *, layer, lam_init):
    c = c_ref[...]
    sc = c * (1.0 / (1.0 + jnp.exp(-c)))
    ada_ref[...] = jnp.dot(sc.astype(BF16), w_ref[...].astype(BF16),
                           preferred_element_type=F32) + b_ref[...]

    @pl.when(pl.program_id(0) == 0)
    def _():
        n_rows = lbl_ref.shape[0]
        rows = [lbl_ref[r:r + 1, :] for r in range(n_rows)]
        mx = functools.reduce(jnp.maximum, rows)
        es = [jnp.exp(r - mx) for r in rows]
        den = functools.reduce(lambda a, b: a + b, es)
        num = functools.reduce(lambda a, b: a + b, es[:layer + 1])
        lb_ref[...] = num / den
        s1 = jnp.sum(q1_ref[...] * k1_ref[...], axis=-1, keepdims=True)
        s2 = jnp.sum(q2_ref[...] * k2_ref[...], axis=-1, keepdims=True)
        lam = jnp.exp(s1) - jnp.exp(s2) + lam_init
        lam_ref[...] = jnp.broadcast_to(lam, lam_ref.shape)


def _prep(c, w_ada, b_ada, lb_logits, lam_q1, lam_k1, lam_q2, lam_k2, *, layer, lam_init):
    B, D = c.shape
    n_out = w_ada.shape[1]
    col = D
    width = lb_logits.shape[1]
    small = lambda shape: pl.BlockSpec(shape, lambda j: (0,) * len(shape))
    return pl.pallas_call(
        functools.partial(_prep_kernel, layer=layer, lam_init=lam_init),
        grid=(n_out // col,),
        in_specs=[
            small((B, D)),
            pl.BlockSpec((D, col), lambda j: (0, j)),
            pl.BlockSpec((1, col), lambda j: (0, j)),
            small(lb_logits.shape),
            small((1, DIFF_DH)), small((1, DIFF_DH)), small((1, DIFF_DH)), small((1, DIFF_DH)),
        ],
        out_specs=[
            pl.BlockSpec((B, col), lambda j: (0, j)),
            small((1, width)),
            small((1, LANES)),
        ],
        out_shape=[
            jax.ShapeDtypeStruct((B, n_out), F32),
            jax.ShapeDtypeStruct((1, width), F32),
            jax.ShapeDtypeStruct((1, LANES), F32),
        ],
        compiler_params=pltpu.CompilerParams(
            dimension_semantics=("arbitrary",),
            vmem_limit_bytes=_vmem_limit(4 * D * col * 4 + (8 << 20))),
        name="prep",
    )(c, w_ada, b_ada, lb_logits, lam_q1, lam_k1, lam_q2, lam_k2)


def _t5_bucket_np(dist):
    max_exact = N_BUCKETS // 2
    d = np.maximum(dist, 1).astype(np.float32)
    large = max_exact + (np.log(d / max_exact) / math.log(MAX_DISTANCE / max_exact)
                         * (N_BUCKETS - max_exact)).astype(np.int32)
    large = np.minimum(large, N_BUCKETS - 1)
    return np.where(dist < max_exact, dist, large)


def _bucket_starts():
    buckets = _t5_bucket_np(np.arange(0, 64 * MAX_DISTANCE))
    assert np.all(np.diff(buckets) >= 0) and set(buckets) == set(range(N_BUCKETS))
    return [int(np.argmax(buckets == b)) for b in range(N_BUCKETS)]


def _bias_reach():
    return _bucket_starts()[-1]


def _num_near_tiles(block):
    t = 0
    while t * block - (block - 1) < _bias_reach():
        t += 1
    return t


def _bias_kernel(tab_ref, o_ref, *, block):
    h = pl.program_id(0)
    t = pl.program_id(1)
    r = lax.broadcasted_iota(jnp.int32, (block, block), 0)
    c = lax.broadcasted_iota(jnp.int32, (block, block), 1)
    dist = r - c + t * block
    bias = jnp.full((block, block), tab_ref[0, h], F32)
    for b, first in enumerate(_bucket_starts()):
        if b:
            bias = jnp.where(dist >= first, tab_ref[b, h], bias)
    shifted = (bias - tab_ref[N_BUCKETS - 1, h]) * LOG2E
    o_ref[0, 0] = jnp.where(dist >= 0, shifted, MASK_VALUE)


def _bias_tiles(rel_bias, *, block, n_tiles):
    H = rel_bias.shape[1]
    return pl.pallas_call(
        functools.partial(_bias_kernel, block=block),
        grid=(H, n_tiles),
        in_specs=[pl.BlockSpec(memory_space=pltpu.SMEM)],
        out_specs=pl.BlockSpec((1, 1, block, block), lambda h, t: (h, t, 0, 0)),
        out_shape=jax.ShapeDtypeStruct((H, n_tiles, block, block), F32),
        compiler_params=pltpu.CompilerParams(dimension_semantics=("arbitrary", "arbitrary")),
        name="bias_tiles",
    )(rel_bias)


def _in_proj_kernel(x_ref, ada_ref, w_ref, o_ref, *, col_scales):
    shift = ada_ref[0, 0:1, :]
    scale = ada_ref[0, 1:2, :]
    n_out = o_ref.shape[1]
    sub = x_ref.shape[0] // PROJ_SUBTILES

    def modulated(t):
        x = x_ref[t * sub:(t + 1) * sub, :]
        return (_layer_norm_rows(x) * (1.0 + scale) + shift).astype(BF16)

    u = modulated(0)
    for t in range(PROJ_SUBTILES):
        u_next = modulated(t + 1) if t + 1 < PROJ_SUBTILES else None
        for c0 in range(0, n_out, PROJ_CHUNK):
            acc = jnp.dot(u, w_ref[:, c0:c0 + PROJ_CHUNK], preferred_element_type=F32)
            for lo, hi, s in col_scales:
                if lo <= c0 < hi:
                    acc = acc * s
            o_ref[t * sub:(t + 1) * sub, c0:c0 + PROJ_CHUNK] = acc.astype(o_ref.dtype)
        u = u_next


def _in_proj(x2, ada3, w_in, *, seq, col_scales):
    N, D = x2.shape
    n_out = w_in.shape[1]
    tm = ROW_TILE * PROJ_SUBTILES
    per_batch = seq // tm
    assert all(lo % PROJ_CHUNK == 0 and hi % PROJ_CHUNK == 0 for lo, hi, _ in col_scales)
    vmem = (2 * tm * D * 4 + D * n_out * 2 + 2 * tm * n_out * 2 + 2 * 8 * D * 4
            + 3 * tm * D * 4 + 2 * tm * PROJ_CHUNK * 4 + (4 << 20))
    return pl.pallas_call(
        functools.partial(_in_proj_kernel, col_scales=col_scales),
        grid=(N // tm,),
        in_specs=[
            pl.BlockSpec((tm, D), lambda i: (i, 0)),
            pl.BlockSpec((1, ada3.shape[1], D), lambda i: (i // per_batch, 0, 0)),
            pl.BlockSpec((D, n_out), lambda i: (0, 0), pipeline_mode=pl.Buffered(1)),
        ],
        out_specs=pl.BlockSpec((tm, n_out), lambda i: (i, 0)),
        out_shape=jax.ShapeDtypeStruct((N, n_out), BF16),
        compiler_params=pltpu.CompilerParams(
            dimension_semantics=("arbitrary",), vmem_limit_bytes=_vmem_limit(vmem)),
        name="in_proj",
    )(x2, ada3, w_in)


def _decay_matrix():
    n = 2 * CHUNK + 16
    r = lax.broadcasted_iota(jnp.int32, (n, CHUNK), 0)
    s = lax.broadcasted_iota(jnp.int32, (n, CHUNK), 1)
    upto_mid = (s < CHUNK // 2).astype(F32)
    d1 = (s <= r).astype(F32) - upto_mid
    d2 = (s > r - CHUNK).astype(F32)
    return jnp.where(r < CHUNK, d1, jnp.where(r < 2 * CHUNK, d2,
                     jnp.where(r < 2 * CHUNK + 8, upto_mid, 1.0))).astype(BF16)


def _split3(x):
    hi = x.astype(BF16)
    r1 = x - hi.astype(F32)
    mid = r1.astype(BF16)
    lo = (r1 - mid.astype(F32)).astype(BF16)
    return hi, mid, lo


def _hgrn_kernel(q_ref, f_ref, i_ref, g_ref, lb_ref, nw_ref, o_ref,
                 st_ref, qd_ref, kd_ref, qin_ref, kst_ref, dl_ref, of_ref):
    @pl.when(pl.program_id(1) == 0)
    def _():
        st_ref[...] = jnp.zeros_like(st_ref)

    lb = lb_ref[...]
    one_m_lb = 1.0 - lb
    nw = nw_ref[...]
    n_chunks = q_ref.shape[1] // CHUNK
    tri_r = lax.broadcasted_iota(jnp.int32, (CHUNK, CHUNK), 0)
    tri_c = lax.broadcasted_iota(jnp.int32, (CHUNK, CHUNK), 1)
    causal = tri_r >= tri_c
    decay = _decay_matrix()
    decay3 = jnp.concatenate([decay, decay, decay], axis=1)
    sub = CHUNK // 8

    half_span = 0.5 * one_m_lb
    f_mid = lb + half_span

    def chunk_rows(n):
        return slice(n * CHUNK, (n + 1) * CHUNK)

    def decay_rows(n):
        return slice(n * 8, (n + 1) * 8)

    def heads():
        return [slice(h * HGRN_DK, (h + 1) * HGRN_DK) for h in range(HGRN_HEADS)]


    def gates(n):
        x = f_ref[0, chunk_rows(n), :].astype(F32)
        th = half_span * jnp.tanh(x)
        lf2 = jnp.log2(f_mid + th)
        return half_span - th, jnp.concatenate(_split3(lf2), axis=0)

    def operands(n, kc, rel):
        rows = chunk_rows(n)
        q = q_ref[0, rows, :].astype(F32)
        d_mid = rel[:CHUNK]
        e1 = jnp.exp2(d_mid)
        qd_f = q * e1
        qd_ref[rows, :] = qd_f.astype(BF16)
        kd_ref[rows, :] = (kc * jnp.exp2(-d_mid)).astype(BF16)
        e_mid = jnp.exp2(rel[2 * CHUNK:2 * CHUNK + 8])
        qin_ref[rows, :] = (qd_f.reshape(sub, 8, -1) * e_mid[None]).reshape(qd_f.shape).astype(BF16)
        kst_ref[rows, :] = (kc * jnp.exp2(rel[CHUNK:2 * CHUNK])).astype(BF16)
        dl_ref[decay_rows(n), :] = jnp.exp2(rel[2 * CHUNK + 8:])

    def prep_group(ns):
        g1 = [gates(n) for n in ns]
        rels = [jnp.dot(decay3, parts, preferred_element_type=F32) for _, parts in g1]
        for n, (kc, _), rel in zip(ns, g1, rels):
            operands(n, kc, rel)

    def mix_group(ns):
        local = []
        for n in ns:
            rows = chunk_rows(n)
            v = i_ref[0, rows, :]
            per_head = []
            for sl in heads():
                a = jnp.where(causal, _dot_nt(qd_ref[rows, sl], kd_ref[rows, sl]), 0.0)
                per_head.append((a.astype(BF16), _dot_tn(v[:, sl], kst_ref[rows, sl])))
            local.append((rows, v, per_head))
        for n, (rows, v, per_head) in zip(ns, local):
            d_last = dl_ref[decay_rows(n), :]
            for h, (sl, (a, upd)) in enumerate(zip(heads(), per_head)):
                st = st_ref[h]
                of_ref[rows, sl] = (jnp.dot(a, v[:, sl], preferred_element_type=F32)
                                    + _dot_nt(qin_ref[rows, sl], st.astype(BF16)))
                st_new = st.reshape(HGRN_DV // 8, 8, HGRN_DK) * d_last[None, :, sl]
                st_ref[h] = st_new.reshape(HGRN_DV, HGRN_DK) + upd

    def norm_block(n):
        rows = chunk_rows(n)
        hg = g_ref[0, rows, :].astype(F32)
        gate = hg + hg * jnp.tanh(hg)
        for sl in heads():
            o = of_ref[rows, sl]
            ms = jnp.mean(o * o, axis=-1, keepdims=True)
            o_ref[0, rows, sl] = (o * lax.rsqrt(ms + RMS_EPS) * nw * gate[:, sl]).astype(o_ref.dtype)

    groups = [list(range(g * HGRN_GROUP, (g + 1) * HGRN_GROUP)) for g in range(n_chunks // HGRN_GROUP)]
    for step in range(len(groups) + 2):
        if step < len(groups):
            prep_group(groups[step])
        if 0 <= step - 1 < len(groups):
            mix_group(groups[step - 1])
        if 0 <= step - 2 < len(groups):
            for n in groups[step - 2]:
                norm_block(n)


def _hgrn(proj3, lb, norm_w):
    B, S, _ = proj3.shape
    ts = HGRN_TILE
    W = HGRN_WIDTH
    spec = lambda j: pl.BlockSpec((1, ts, W), lambda b, s, j=j: (b, s, j))
    vmem = 5 * 2 * ts * W * 2 + HGRN_HEADS * HGRN_DK * HGRN_DV * 4 + (16 << 20)
    return pl.pallas_call(
        _hgrn_kernel,
        grid=(B, S // ts),
        in_specs=[spec(0), spec(1), spec(2), spec(3),
                  pl.BlockSpec((1, W), lambda b, s: (0, 0)),
                  pl.BlockSpec((1, HGRN_DV), lambda b, s: (0, 0))],
        out_specs=pl.BlockSpec((1, ts, W), lambda b, s: (b, s, 0)),
        out_shape=jax.ShapeDtypeStruct((B, S, W), BF16),
        scratch_shapes=[pltpu.VMEM((HGRN_HEADS, HGRN_DV, HGRN_DK), F32),
                        pltpu.VMEM((ts, W), BF16), pltpu.VMEM((ts, W), BF16),
                        pltpu.VMEM((ts, W), BF16), pltpu.VMEM((ts, W), BF16),
                        pltpu.VMEM((ts // CHUNK * 8, W), F32),
                        pltpu.VMEM((ts, W), F32)],
        compiler_params=pltpu.CompilerParams(
            dimension_semantics=("arbitrary", "arbitrary"), vmem_limit_bytes=_vmem_limit(vmem)),
        name="hgrn",
    )(proj3, proj3, proj3, proj3, lb, norm_w)


def _attn_schedule(nq):
    pairs = [(qi, kj) for qi in range(1, nq) for kj in range(qi)]
    nxt = pairs[1:] + pairs[-1:]
    return np.array([[p[0] for p in nxt], [p[1] for p in nxt],
                     [p[0] for p in pairs], [p[1] for p in pairs]], np.int32)


def _attn_kernel(sched_ref, q_ref, k_ref, v_ref, diag_ref, near_ref, lam_ref, nw_ref, o_ref,
                 s0_ref, s1_ref, m_ref, acc_ref, *, block, out_scale):
    tq = block
    half = tq // 2
    nq = q_ref.shape[1] // tq
    n_off = sched_ref.shape[1]
    ones = jnp.ones((block, LANES), BF16)
    lam = lam_ref[...]

    def blk(j):
        return pl.ds(pl.multiple_of(j * block, block), block)

    def stacked_queries(qi):
        q12 = q_ref[0, blk(qi), :]
        lane = lax.broadcasted_iota(jnp.int32, q12.shape, 1)
        zero = jnp.zeros_like(q12)
        q1 = jnp.where(lane < DIFF_DH, q12, zero)
        q2 = jnp.where(lane >= DIFF_DH, q12, zero)
        return jnp.concatenate([q1[:half], q2[:half], q1[half:], q2[half:]], axis=0)

    def values(kj):
        return jnp.concatenate([v_ref[0, blk(kj), :], ones], axis=1)

    def online_softmax(s, rows, qi, v_ext):
        n_lane_tiles = s.shape[1] // LANES
        mx = s[:, :LANES]
        for c in range(1, n_lane_tiles):
            mx = jnp.maximum(mx, s[:, c * LANES:(c + 1) * LANES])
        m_prev = m_ref[qi, rows, :]
        m_new = jnp.maximum(m_prev, jnp.max(mx, axis=-1, keepdims=True))
        alpha = jnp.exp2(m_prev - m_new)
        p = jnp.exp2(s - jnp.concatenate([m_new] * n_lane_tiles, axis=1)).astype(BF16)
        pv = jnp.dot(p, v_ext, preferred_element_type=F32)
        acc_ref[qi, rows, :] = jnp.concatenate([alpha, alpha], axis=1) * acc_ref[qi, rows, :] + pv
        m_ref[qi, rows, :] = m_new

    @pl.when((pl.program_id(0) == 0) & (pl.program_id(1) == 0))
    def _():
        m_ref[...] = jnp.full_like(m_ref, MASK_VALUE)
        acc_ref[...] = jnp.zeros_like(acc_ref)

    def off_scores(qi, kj, s_ref):
        s_ref[...] = _dot_nt(stacked_queries(qi), k_ref[0, blk(kj), :])

    def off_update(step, s_ref):
        qi = sched_ref[2, step]
        kj = sched_ref[3, step]
        corner = jnp.where(qi - kj == 1, 1.0, 0.0) * near_ref[0, 0]
        for r0 in (0, half):
            s_ref[r0:r0 + LANES, tq - LANES:] += corner
        online_softmax(s_ref[...], slice(0, 2 * tq), qi, values(kj))

    unroll = max(u for u in range(2, ATT_UNROLL + 1, 2) if n_off % u == 0)

    def off_steps(i, carry):
        for j in range(0, unroll, 2):
            step = unroll * i + j
            off_scores(sched_ref[0, step], sched_ref[1, step], s1_ref)
            off_update(step, s0_ref)
            off_scores(sched_ref[0, step + 1], sched_ref[1, step + 1], s0_ref)
            off_update(step + 1, s1_ref)
        return carry

    off_scores(1, 0, s0_ref)
    lax.fori_loop(0, n_off // unroll, off_steps, 0)

    def diag_scores(qi, s_ref):
        qz = stacked_queries(qi)
        s_ref[:tq, :half] = _dot_nt(qz[:tq], k_ref[0, pl.ds(pl.multiple_of(qi * tq, tq), half), :])
        s_ref[tq:, :] = _dot_nt(qz[tq:], k_ref[0, blk(qi), :])

    def diag_update(qi, s_ref):
        bias = diag_ref[0, 0]
        v_ext = values(qi)
        bias_a = bias[:half, :half]
        bias_b = bias[half:, :]
        online_softmax(s_ref[:tq, :half] + jnp.concatenate([bias_a, bias_a], axis=0),
                       slice(0, tq), qi, v_ext[:half])
        online_softmax(s_ref[tq:, :] + jnp.concatenate([bias_b, bias_b], axis=0),
                       slice(tq, 2 * tq), qi, v_ext)
        acc = acc_ref[qi]
        o_all = acc[:, :DIFF_DV] / acc[:, DIFF_DV:]
        o = jnp.concatenate([o_all[:half] - lam * o_all[half:tq],
                             o_all[tq:tq + half] - lam * o_all[tq + half:]], axis=0)
        ms = jnp.mean(o * o, axis=-1, keepdims=True)
        y = o * lax.rsqrt(ms + RMS_EPS) * nw_ref[...] * out_scale
        o_ref[0, blk(qi), :] = y.astype(o_ref.dtype)
        m_ref[qi] = jnp.full((2 * tq, LANES), MASK_VALUE, F32)
        acc_ref[qi] = jnp.zeros_like(acc)

    def diag_pair(i, carry):
        diag_scores(jnp.minimum(2 * i + 1, nq - 1), s1_ref)
        diag_update(2 * i, s0_ref)
        diag_scores(jnp.minimum(2 * i + 2, nq - 1), s0_ref)
        diag_update(2 * i + 1, s1_ref)
        return carry

    diag_scores(0, s0_ref)
    lax.fori_loop(0, nq // 2, diag_pair, 0)


def _attn(proj3, bias_tiles, lam, norm_w, *, out_scale):
    B, S, _ = proj3.shape
    H = DIFF_HEADS
    block = ATT_BLOCK
    nq = S // block
    sched = _attn_schedule(nq)
    assert nq % 2 == 0 and sched.shape[1] % 2 == 0, "pipeline steps are unrolled in pairs"
    assert bias_tiles.shape[1] == 2 and _bias_reach() <= LANES <= block // 2
    q_blk = 4 * HGRN_HEADS
    k_blk = q_blk + H
    v_blk = k_blk + H
    seq_spec = lambda j: pl.BlockSpec((1, S, LANES), lambda b, h, j=j: (b, 0, j + h))
    rows = 2 * block
    scratch = [pltpu.VMEM((rows, block), F32), pltpu.VMEM((rows, block), F32),
               pltpu.VMEM((nq, rows, LANES), F32),
               pltpu.VMEM((nq, rows, 2 * LANES), F32)]
    scratch_bytes = rows * (2 * block * 4 + nq * 3 * LANES * 4)
    vmem = (4 * 2 * S * LANES * 2 + 2 * block * block * 4 + scratch_bytes
            + 5 * rows * block * 4 + (6 << 20))
    return pl.pallas_call(
        functools.partial(_attn_kernel, block=block, out_scale=out_scale),
        grid=(B, H),
        in_specs=[
            pl.BlockSpec(memory_space=pltpu.SMEM),
            seq_spec(q_blk), seq_spec(k_blk), seq_spec(v_blk),
            pl.BlockSpec((1, 1, block, block), lambda b, h: (h, 0, 0, 0)),
            pl.BlockSpec((1, 1, LANES, LANES), lambda b, h: (h, 1, 0, block // LANES - 1)),
            pl.BlockSpec((1, LANES), lambda b, h: (0, 0)),
            pl.BlockSpec((1, DIFF_DV), lambda b, h: (0, 0)),
        ],
        out_specs=pl.BlockSpec((1, S, LANES), lambda b, h: (b, 0, h)),
        out_shape=jax.ShapeDtypeStruct((B, S, DIFF_WIDTH), BF16),
        scratch_shapes=scratch,
        compiler_params=pltpu.CompilerParams(
            dimension_semantics=("arbitrary", "arbitrary"),
            vmem_limit_bytes=_vmem_limit(vmem)),
        name="attn",
    )(jnp.asarray(sched), proj3, proj3, proj3, bias_tiles, bias_tiles, lam, norm_w)


def _out_ffn_kernel(x_ref, oh_ref, od_ref, ada_ref, wo_ref, g1_ref, b1_ref,
                    wg_ref, wu_ref, wd_ref, g2_ref, b2_ref, o_ref, x1_ref, u_ref, *, alpha):
    gate_m = ada_ref[0, 2:3, :]
    shift_f = ada_ref[0, 3:4, :]
    scale_f = ada_ref[0, 4:5, :]
    gate_f = ada_ref[0, 5:6, :]
    hw = oh_ref.shape[1]
    sub = x_ref.shape[0] // FFN_SUBTILES
    n_chunks = wg_ref.shape[1] // FF_CHUNK

    def rows_of(t):
        return slice(t * sub, (t + 1) * sub)

    def prep(rows):
        mix = (jnp.dot(oh_ref[rows, :], wo_ref[:hw, :], preferred_element_type=F32)
               + jnp.dot(od_ref[rows, :], wo_ref[hw:, :], preferred_element_type=F32))
        x1 = (_layer_norm_rows(alpha * x_ref[rows, :] + (1.0 + gate_m) * mix) * g1_ref[...]
              + b1_ref[...])
        x1_ref[rows, :] = x1
        u_ref[rows, :] = (_layer_norm_rows(x1) * (1.0 + scale_f) + shift_f).astype(BF16)

    def swiglu(rows):
        u = u_ref[rows, :]
        y = jnp.zeros((sub, o_ref.shape[1]), F32)
        for c in range(n_chunks):
            cols = slice(c * FF_CHUNK, (c + 1) * FF_CHUNK)
            a = jnp.dot(u, wg_ref[:, cols], preferred_element_type=F32)
            b = jnp.dot(u, wu_ref[:, cols], preferred_element_type=F32)
            hid = (a * (1.0 / (1.0 + jnp.exp(-a))) * b).astype(BF16)
            y = y + jnp.dot(hid, wd_ref[cols, :], preferred_element_type=F32)
        return y

    def finish(rows, y):
        o_ref[rows, :] = (_layer_norm_rows(alpha * x1_ref[rows, :] + (1.0 + gate_f) * y)
                          * g2_ref[...] + b2_ref[...])

    prep(rows_of(0))
    for t in range(FFN_SUBTILES):
        if t + 1 < FFN_SUBTILES:
            prep(rows_of(t + 1))
        y = swiglu(rows_of(t))
        finish(rows_of(t), y)


def _out_ffn(x2, oh2, od2, ada3, w_out, ln1_g, ln1_b, w_gate, w_up, w_down, ln2_g, ln2_b,
             *, seq, alpha):
    N, D = x2.shape
    tm = ROW_TILE * FFN_SUBTILES
    sub = ROW_TILE
    per_batch = seq // tm
    row = lambda w: pl.BlockSpec((tm, w), lambda i: (i, 0))
    const = lambda shape: pl.BlockSpec(shape, lambda i: (0, 0), pipeline_mode=pl.Buffered(1))
    vec = pl.BlockSpec((1, D), lambda i: (0, 0))
    weights = (w_out.size + w_gate.size + w_up.size + w_down.size) * 2
    vmem = (weights + 2 * 2 * tm * D * 4 + 2 * 2 * tm * oh2.shape[1] * 2 + tm * D * (4 + 2)
            + 4 * sub * D * 4 + 4 * sub * FF_CHUNK * 4 + (4 << 20))
    return pl.pallas_call(
        functools.partial(_out_ffn_kernel, alpha=alpha),
        grid=(N // tm,),
        in_specs=[
            row(D), row(oh2.shape[1]), row(od2.shape[1]),
            pl.BlockSpec((1, ada3.shape[1], D), lambda i: (i // per_batch, 0, 0)),
            const(w_out.shape), vec, vec,
            const(w_gate.shape), const(w_up.shape), const(w_down.shape), vec, vec,
        ],
        out_specs=row(D),
        out_shape=jax.ShapeDtypeStruct((N, D), F32),
        scratch_shapes=[pltpu.VMEM((tm, D), F32), pltpu.VMEM((tm, D), BF16)],
        compiler_params=pltpu.CompilerParams(
            dimension_semantics=("arbitrary",), vmem_limit_bytes=_vmem_limit(vmem)),
        name="out_ffn",
    )(x2, oh2, od2, ada3, w_out, ln1_g, ln1_b, w_gate, w_up, w_down, ln2_g, ln2_b)


def kernel(x, c, w_ada, b_ada, w_in, lb_logits, hgrn_norm_w, lam_q1, lam_k1, lam_q2, lam_k2,
           diff_norm_w, rel_bias, w_out, ln1_g, ln1_b, w_gate, w_up, w_down, ln2_g, ln2_b):
    B, S, D = x.shape
    depth = w_ada.shape[0]
    assert S % (ROW_TILE * max(FFN_SUBTILES, PROJ_SUBTILES)) == 0
    assert S % HGRN_TILE == 0 and S % ATT_BLOCK == 0
    assert HGRN_TILE % (CHUNK * HGRN_GROUP) == 0 and w_in.shape[2] % PROJ_CHUNK == 0
    assert w_gate.shape[2] % FF_CHUNK == 0
    alpha = (2.0 * depth) ** 0.25
    bias_tiles = _bias_tiles(rel_bias, block=ATT_BLOCK, n_tiles=_num_near_tiles(ATT_BLOCK))
    col_scales = ((0, HGRN_WIDTH, HGRN_DK ** -0.5),
                  (HGRN_WIDTH, 2 * HGRN_WIDTH, 0.5),
                  (3 * HGRN_WIDTH, 4 * HGRN_WIDTH, 0.5),
                  (4 * HGRN_WIDTH, 4 * HGRN_WIDTH + DIFF_WIDTH, LOG2E * DIFF_DH ** -0.5))
    for l in range(depth):
        lam_init = 0.8 - 0.6 * math.exp(-0.3 * l)
        ada, lb, lam = _prep(c, w_ada[l], b_ada[l][None, :], lb_logits,
                             lam_q1[l][None, :], lam_k1[l][None, :],
                             lam_q2[l][None, :], lam_k2[l][None, :],
                             layer=l, lam_init=lam_init)
        ada3 = ada.reshape(B, 6, D)
        x2 = x.reshape(B * S, D)
        proj = _in_proj(x2, ada3, w_in[l].astype(BF16), seq=S, col_scales=col_scales)
        proj3 = proj.reshape(B, S, proj.shape[1])
        o_h = _hgrn(proj3, lb, hgrn_norm_w[l][None, :])
        o_d = _attn(proj3, bias_tiles, lam, diff_norm_w[l][None, :],
                    out_scale=1.0 - lam_init)
        out = _out_ffn(x2, o_h.reshape(B * S, -1), o_d.reshape(B * S, -1), ada3,
                       w_out[l].astype(BF16), ln1_g[l][None, :], ln1_b[l][None, :],
                       w_gate[l].astype(BF16), w_up[l].astype(BF16), w_down[l].astype(BF16),
                       ln2_g[l][None, :], ln2_b[l][None, :], seq=S, alpha=alpha)
        x = out.reshape(B, S, D)
    return x
```

```python
import functools
import math

import jax
import jax.numpy as jnp
import numpy as np
from jax import lax
from jax.experimental import pallas as pl
from jax.experimental.pallas import tpu as pltpu

HGRN_HEADS = 4
HGRN_DK = 128
HGRN_DV = 128
HGRN_WIDTH = HGRN_HEADS * HGRN_DV
DIFF_HEADS = 4
DIFF_DH = 64
DIFF_DV = 2 * DIFF_DH
DIFF_WIDTH = DIFF_HEADS * DIFF_DV
N_BUCKETS = 32
MAX_DISTANCE = 128
CHUNK = 64
LN_EPS = 1e-5
RMS_EPS = 1e-6
LOG2E = math.log2(math.e)

LANES = 128
SUBLANES = 8
V7X_VMEM_BYTES = 64 * 1024 * 1024

ROW_TILE = 512
HGRN_TILE = 1024
HGRN_GROUP = 2
ATT_BLOCK = 512
ATT_UNROLL = 14
FF_CHUNK = 256
FFN_SUBTILES = 2
PROJ_SUBTILES = 2
PROJ_CHUNK = 512
MASK_VALUE = -1e30

BF16 = jnp.bfloat16
F32 = jnp.float32


def _vmem_limit(block_bytes):
    return int(min(block_bytes, V7X_VMEM_BYTES - 4 * 1024 * 1024))


def _layer_norm_rows(x):
    mu = jnp.mean(x, axis=-1, keepdims=True)
    xc = x - mu
    var = jnp.mean(xc * xc, axis=-1, keepdims=True)
    return xc * lax.rsqrt(var + LN_EPS)


def _dot_nt(a, b):
    return lax.dot_general(a, b, (((1,), (1,)), ((), ())), preferred_element_type=F32)


def _dot_tn(a, b):
    return lax.dot_general(a, b, (((0,), (0,)), ((), ())), preferred_element_type=F32)


def _prep_kernel(c_ref, w_ref, b_ref, lbl_ref, q1_ref, k1_ref, q2_ref, k2_ref,
                 ada_ref, lb_ref, lam_ref, *, layer, lam_init):
    c = c_ref[...]
    sc = c * (1.0 / (1.0 + jnp.exp(-c)))
    ada_ref[...] = jnp.dot(sc.astype(BF16), w_ref[...].astype(BF16),
                           preferred_element_type=F32) + b_ref[...]

    @pl.when(pl.program_id(0) == 0)
    def _():
        n_rows = lbl_ref.shape[0]
        rows = [lbl_ref[r:r + 1, :] for r in range(n_rows)]
        mx = functools.reduce(jnp.maximum, rows)
        es = [jnp.exp(r - mx) for r in rows]
        den = functools.reduce(lambda a, b: a + b, es)
        num = functools.reduce(lambda a, b: a + b, es[:layer + 1])
        lb_ref[...] = num / den
        s1 = jnp.sum(q1_ref[...] * k1_ref[...], axis=-1, keepdims=True)
        s2 = jnp.sum(q2_ref[...] * k2_ref[...], axis=-1, keepdims=True)
        lam = jnp.exp(s1) - jnp.exp(s2) + lam_init
        lam_ref[...] = jnp.broadcast_to(lam, lam_ref.shape)


def _prep(c, w_ada, b_ada, lb_logits, lam_q1, lam_k1, lam_q2, lam_k2, *, layer, lam_init):
    B, D = c.shape
    n_out = w_ada.shape[1]
    col = D
    width = lb_logits.shape[1]
    small = lambda shape: pl.BlockSpec(shape, lambda j: (0,) * len(shape))
    return pl.pallas_call(
        functools.partial(_prep_kernel, layer=layer, lam_init=lam_init),
        grid=(n_out // col,),
        in_specs=[
            small((B, D)),
            pl.BlockSpec((D, col), lambda j: (0, j)),
            pl.BlockSpec((1, col), lambda j: (0, j)),
            small(lb_logits.shape),
            small((1, DIFF_DH)), small((1, DIFF_DH)), small((1, DIFF_DH)), small((1, DIFF_DH)),
        ],
        out_specs=[
            pl.BlockSpec((B, col), lambda j: (0, j)),
            small((1, width)),
            small((1, LANES)),
        ],
        out_shape=[
            jax.ShapeDtypeStruct((B, n_out), F32),
            jax.ShapeDtypeStruct((1, width), F32),
            jax.ShapeDtypeStruct((1, LANES), F32),
        ],
        compiler_params=pltpu.CompilerParams(
            dimension_semantics=("arbitrary",),
            vmem_limit_bytes=_vmem_limit(4 * D * col * 4 + (8 << 20))),
        name="prep",
    )(c, w_ada, b_ada, lb_logits, lam_q1, lam_k1, lam_q2, lam_k2)


def _t5_bucket_np(dist):
    max_exact = N_BUCKETS // 2
    d = np.maximum(dist, 1).astype(np.float32)
    large = max_exact + (np.log(d / max_exact) / math.log(MAX_DISTANCE / max_exact)
                         * (N_BUCKETS - max_exact)).astype(np.int32)
    large = np.minimum(large, N_BUCKETS - 1)
    return np.where(dist < max_exact, dist, large)


def _bucket_starts():
    buckets = _t5_bucket_np(np.arange(0, 64 * MAX_DISTANCE))
    assert np.all(np.diff(buckets) >= 0) and set(buckets) == set(range(N_BUCKETS))
    return [int(np.argmax(buckets == b)) for b in range(N_BUCKETS)]


def _bias_reach():
    return _bucket_starts()[-1]


def _num_near_tiles(block):
    t = 0
    while t * block - (block - 1) < _bias_reach():
        t += 1
    return t


def _bias_kernel(tab_ref, o_ref, *, block):
    h = pl.program_id(0)
    t = pl.program_id(1)
    r = lax.broadcasted_iota(jnp.int32, (block, block), 0)
    c = lax.broadcasted_iota(jnp.int32, (block, block), 1)
    dist = r - c + t * block
    bias = jnp.full((block, block), tab_ref[0, h], F32)
    for b, first in enumerate(_bucket_starts()):
        if b:
            bias = jnp.where(dist >= first, tab_ref[b, h], bias)
    shifted = (bias - tab_ref[N_BUCKETS - 1, h]) * LOG2E
    o_ref[0, 0] = jnp.where(dist >= 0, shifted, MASK_VALUE)


def _bias_tiles(rel_bias, *, block, n_tiles):
    H = rel_bias.shape[1]
    return pl.pallas_call(
        functools.partial(_bias_kernel, block=block),
        grid=(H, n_tiles),
        in_specs=[pl.BlockSpec(memory_space=pltpu.SMEM)],
        out_specs=pl.BlockSpec((1, 1, block, block), lambda h, t: (h, t, 0, 0)),
        out_shape=jax.ShapeDtypeStruct((H, n_tiles, block, block), F32),
        compiler_params=pltpu.CompilerParams(dimension_semantics=("arbitrary", "arbitrary")),
        name="bias_tiles",
    )(rel_bias)


def _in_proj_kernel(x_ref, ada_ref, w_ref, o_ref, *, col_scales):
    shift = ada_ref[0, 0:1, :]
    scale = ada_ref[0, 1:2, :]
    n_out = o_ref.shape[1]
    sub = x_ref.shape[0] // PROJ_SUBTILES

    def modulated(t):
        x = x_ref[t * sub:(t + 1) * sub, :]
        return (_layer_norm_rows(x) * (1.0 + scale) + shift).astype(BF16)

    u = modulated(0)
    for t in range(PROJ_SUBTILES):
        u_next = modulated(t + 1) if t + 1 < PROJ_SUBTILES else None
        for c0 in range(0, n_out, PROJ_CHUNK):
            acc = jnp.dot(u, w_ref[:, c0:c0 + PROJ_CHUNK], preferred_element_type=F32)
            for lo, hi, s in col_scales:
                if lo <= c0 < hi:
                    acc = acc * s
            o_ref[t * sub:(t + 1) * sub, c0:c0 + PROJ_CHUNK] = acc.astype(o_ref.dtype)
        u = u_next


def _in_proj(x2, ada3, w_in, *, seq, col_scales):
    N, D = x2.shape
    n_out = w_in.shape[1]
    tm = ROW_TILE * PROJ_SUBTILES
    per_batch = seq // tm
    assert all(lo % PROJ_CHUNK == 0 and hi % PROJ_CHUNK == 0 for lo, hi, _ in col_scales)
    vmem = (2 * tm * D * 4 + D * n_out * 2 + 2 * tm * n_out * 2 + 2 * 8 * D * 4
            + 3 * tm * D * 4 + 2 * tm * PROJ_CHUNK * 4 + (4 << 20))
    return pl.pallas_call(
        functools.partial(_in_proj_kernel, col_scales=col_scales),
        grid=(N // tm,),
        in_specs=[
            pl.BlockSpec((tm, D), lambda i: (i, 0)),
            pl.BlockSpec((1, ada3.shape[1], D), lambda i: (i // per_batch, 0, 0)),
            pl.BlockSpec((D, n_out), lambda i: (0, 0), pipeline_mode=pl.Buffered(1)),
        ],
        out_specs=pl.BlockSpec((tm, n_out), lambda i: (i, 0)),
        out_shape=jax.ShapeDtypeStruct((N, n_out), BF16),
        compiler_params=pltpu.CompilerParams(
            dimension_semantics=("arbitrary",), vmem_limit_bytes=_vmem_limit(vmem)),
        name="in_proj",
    )(x2, ada3, w_in)


def _decay_matrix():
    n = 2 * CHUNK + 2 * SUBLANES
    r = lax.broadcasted_iota(jnp.int32, (n, CHUNK), 0)
    s = lax.broadcasted_iota(jnp.int32, (n, CHUNK), 1)
    upto_mid = (s < CHUNK // 2).astype(F32)
    d1 = (s <= r).astype(F32) - upto_mid
    d2 = (s > r - CHUNK).astype(F32)
    return jnp.where(r < CHUNK, d1, jnp.where(r < 2 * CHUNK, d2,
                     jnp.where(r < 2 * CHUNK + SUBLANES, upto_mid, 1.0))).astype(BF16)


def _split3(x):
    hi = x.astype(BF16)
    r1 = x - hi.astype(F32)
    mid = r1.astype(BF16)
    lo = (r1 - mid.astype(F32)).astype(BF16)
    return hi, mid, lo


def _hgrn_kernel(q_ref, f_ref, i_ref, g_ref, lb_ref, nw_ref, o_ref,
                 st_ref, qd_ref, kd_ref, qin_ref, kst_ref, dl_ref, of_ref):
    @pl.when(pl.program_id(1) == 0)
    def _():
        st_ref[...] = jnp.zeros_like(st_ref)

    lb = lb_ref[...]
    one_m_lb = 1.0 - lb
    nw = nw_ref[...]
    n_chunks = q_ref.shape[1] // CHUNK
    tri_r = lax.broadcasted_iota(jnp.int32, (CHUNK, CHUNK), 0)
    tri_c = lax.broadcasted_iota(jnp.int32, (CHUNK, CHUNK), 1)
    causal = tri_r >= tri_c
    decay = _decay_matrix()
    decay3 = jnp.concatenate([decay, decay, decay], axis=1)
    sub = CHUNK // SUBLANES

    half_span = 0.5 * one_m_lb
    f_mid = lb + half_span

    def chunk_rows(n):
        return slice(n * CHUNK, (n + 1) * CHUNK)

    def decay_rows(n):
        return slice(n * SUBLANES, (n + 1) * SUBLANES)

    def heads():
        return [slice(h * HGRN_DK, (h + 1) * HGRN_DK) for h in range(HGRN_HEADS)]


    def gates(n):
        x = f_ref[0, chunk_rows(n), :].astype(F32)
        th = half_span * jnp.tanh(x)
        lf2 = jnp.log2(f_mid + th)
        return half_span - th, jnp.concatenate(_split3(lf2), axis=0)

    def operands(n, kc, rel):
        rows = chunk_rows(n)
        q = q_ref[0, rows, :].astype(F32)
        d_mid = rel[:CHUNK]
        e1 = jnp.exp2(d_mid)
        qd_f = q * e1
        qd_ref[rows, :] = qd_f.astype(BF16)
        kd_ref[rows, :] = (kc * jnp.exp2(-d_mid)).astype(BF16)
        e_mid = jnp.exp2(rel[2 * CHUNK:2 * CHUNK + SUBLANES])
        qin_ref[rows, :] = (qd_f.reshape(sub, SUBLANES, -1) * e_mid[None]).reshape(qd_f.shape).astype(BF16)
        kst_ref[rows, :] = (kc * jnp.exp2(rel[CHUNK:2 * CHUNK])).astype(BF16)
        dl_ref[decay_rows(n), :] = jnp.exp2(rel[2 * CHUNK + SUBLANES:])

    def prep_group(ns):
        g1 = [gates(n) for n in ns]
        rels = [jnp.dot(decay3, parts, preferred_element_type=F32) for _, parts in g1]
        for n, (kc, _), rel in zip(ns, g1, rels):
            operands(n, kc, rel)

    def mix_group(ns):
        local = []
        for n in ns:
            rows = chunk_rows(n)
            v = i_ref[0, rows, :]
            per_head = []
            for sl in heads():
                a = jnp.where(causal, _dot_nt(qd_ref[rows, sl], kd_ref[rows, sl]), 0.0)
                per_head.append((a.astype(BF16), _dot_tn(v[:, sl], kst_ref[rows, sl])))
            local.append((rows, v, per_head))
        for n, (rows, v, per_head) in zip(ns, local):
            d_last = dl_ref[decay_rows(n), :]
            for h, (sl, (a, upd)) in enumerate(zip(heads(), per_head)):
                st = st_ref[h]
                of_ref[rows, sl] = (jnp.dot(a, v[:, sl], preferred_element_type=F32)
                                    + _dot_nt(qin_ref[rows, sl], st.astype(BF16)))
                st_new = st.reshape(HGRN_DV // SUBLANES, SUBLANES, HGRN_DK) * d_last[None, :, sl]
                st_ref[h] = st_new.reshape(HGRN_DV, HGRN_DK) + upd

    def norm_block(n):
        rows = chunk_rows(n)
        hg = g_ref[0, rows, :].astype(F32)
        gate = hg + hg * jnp.tanh(hg)
        for sl in heads():
            o = of_ref[rows, sl]
            ms = jnp.mean(o * o, axis=-1, keepdims=True)
            o_ref[0, rows, sl] = (o * lax.rsqrt(ms + RMS_EPS) * nw * gate[:, sl]).astype(o_ref.dtype)

    groups = [list(range(g * HGRN_GROUP, (g + 1) * HGRN_GROUP)) for g in range(n_chunks // HGRN_GROUP)]
    for step in range(len(groups) + 2):
        if step < len(groups):
            prep_group(groups[step])
        if 0 <= step - 1 < len(groups):
            mix_group(groups[step - 1])
        if 0 <= step - 2 < len(groups):
            for n in groups[step - 2]:
                norm_block(n)


def _hgrn(proj3, lb, norm_w):
    B, S, _ = proj3.shape
    ts = HGRN_TILE
    W = HGRN_WIDTH
    spec = lambda j: pl.BlockSpec((1, ts, W), lambda b, s, j=j: (b, s, j))
    vmem = 5 * 2 * ts * W * 2 + HGRN_HEADS * HGRN_DK * HGRN_DV * 4 + (16 << 20)
    return pl.pallas_call(
        _hgrn_kernel,
        grid=(B, S // ts),
        in_specs=[spec(0), spec(1), spec(2), spec(3),
                  pl.BlockSpec((1, W), lambda b, s: (0, 0)),
                  pl.BlockSpec((1, HGRN_DV), lambda b, s: (0, 0))],
        out_specs=pl.BlockSpec((1, ts, W), lambda b, s: (b, s, 0)),
        out_shape=jax.ShapeDtypeStruct((B, S, W), BF16),
        scratch_shapes=[pltpu.VMEM((HGRN_HEADS, HGRN_DV, HGRN_DK), F32),
                        pltpu.VMEM((ts, W), BF16), pltpu.VMEM((ts, W), BF16),
                        pltpu.VMEM((ts, W), BF16), pltpu.VMEM((ts, W), BF16),
                        pltpu.VMEM((ts // CHUNK * SUBLANES, W), F32),
                        pltpu.VMEM((ts, W), F32)],
        compiler_params=pltpu.CompilerParams(
            dimension_semantics=("arbitrary", "arbitrary"), vmem_limit_bytes=_vmem_limit(vmem)),
        name="hgrn",
    )(proj3, proj3, proj3, proj3, lb, norm_w)


def _attn_schedule(nq):
    pairs = [(qi, kj) for qi in range(1, nq) for kj in range(qi)]
    nxt = pairs[1:] + pairs[-1:]
    return np.array([[p[0] for p in nxt], [p[1] for p in nxt],
                     [p[0] for p in pairs], [p[1] for p in pairs]], np.int32)


def _attn_kernel(sched_ref, q_ref, k_ref, v_ref, diag_ref, near_ref, lam_ref, nw_ref, o_ref,
                 s0_ref, s1_ref, m_ref, acc_ref, *, block, out_scale):
    tq = block
    half = tq // 2
    nq = q_ref.shape[1] // tq
    n_off = sched_ref.shape[1]
    ones = jnp.ones((block, LANES), BF16)
    lam = lam_ref[...]

    def blk(j):
        return pl.ds(pl.multiple_of(j * block, block), block)

    def stacked_queries(qi):
        q12 = q_ref[0, blk(qi), :]
        lane = lax.broadcasted_iota(jnp.int32, q12.shape, 1)
        zero = jnp.zeros_like(q12)
        q1 = jnp.where(lane < DIFF_DH, q12, zero)
        q2 = jnp.where(lane >= DIFF_DH, q12, zero)
        return jnp.concatenate([q1[:half], q2[:half], q1[half:], q2[half:]], axis=0)

    def values(kj):
        return jnp.concatenate([v_ref[0, blk(kj), :], ones], axis=1)

    def online_softmax(s, rows, qi, v_ext):
        n_lane_tiles = s.shape[1] // LANES
        mx = s[:, :LANES]
        for c in range(1, n_lane_tiles):
            mx = jnp.maximum(mx, s[:, c * LANES:(c + 1) * LANES])
        m_prev = m_ref[qi, rows, :]
        m_new = jnp.maximum(m_prev, jnp.max(mx, axis=-1, keepdims=True))
        alpha = jnp.exp2(m_prev - m_new)
        p = jnp.exp2(s - jnp.concatenate([m_new] * n_lane_tiles, axis=1)).astype(BF16)
        pv = jnp.dot(p, v_ext, preferred_element_type=F32)
        acc_ref[qi, rows, :] = jnp.concatenate([alpha, alpha], axis=1) * acc_ref[qi, rows, :] + pv
        m_ref[qi, rows, :] = m_new

    @pl.when((pl.program_id(0) == 0) & (pl.program_id(1) == 0))
    def _():
        m_ref[...] = jnp.full_like(m_ref, MASK_VALUE)
        acc_ref[...] = jnp.zeros_like(acc_ref)

    def off_scores(qi, kj, s_ref):
        s_ref[...] = _dot_nt(stacked_queries(qi), k_ref[0, blk(kj), :])

    def off_update(step, s_ref):
        qi = sched_ref[2, step]
        kj = sched_ref[3, step]
        corner = jnp.where(qi - kj == 1, 1.0, 0.0) * near_ref[0, 0]
        for r0 in (0, half):
            s_ref[r0:r0 + LANES, tq - LANES:] += corner
        online_softmax(s_ref[...], slice(0, 2 * tq), qi, values(kj))

    unroll = max(u for u in range(2, ATT_UNROLL + 1, 2) if n_off % u == 0)

    def off_steps(i, carry):
        for j in range(0, unroll, 2):
            step = unroll * i + j
            off_scores(sched_ref[0, step], sched_ref[1, step], s1_ref)
            off_update(step, s0_ref)
            off_scores(sched_ref[0, step + 1], sched_ref[1, step + 1], s0_ref)
            off_update(step + 1, s1_ref)
        return carry

    off_scores(1, 0, s0_ref)
    lax.fori_loop(0, n_off // unroll, off_steps, 0)

    def diag_scores(qi, s_ref):
        qz = stacked_queries(qi)
        s_ref[:tq, :half] = _dot_nt(qz[:tq], k_ref[0, pl.ds(pl.multiple_of(qi * tq, tq), half), :])
        s_ref[tq:, :] = _dot_nt(qz[tq:], k_ref[0, blk(qi), :])

    def diag_update(qi, s_ref):
        bias = diag_ref[0, 0]
        v_ext = values(qi)
        bias_a = bias[:half, :half]
        bias_b = bias[half:, :]
        online_softmax(s_ref[:tq, :half] + jnp.concatenate([bias_a, bias_a], axis=0),
                       slice(0, tq), qi, v_ext[:half])
        online_softmax(s_ref[tq:, :] + jnp.concatenate([bias_b, bias_b], axis=0),
                       slice(tq, 2 * tq), qi, v_ext)
        acc = acc_ref[qi]
        o_all = acc[:, :DIFF_DV] / acc[:, DIFF_DV:]
        o = jnp.concatenate([o_all[:half] - lam * o_all[half:tq],
                             o_all[tq:tq + half] - lam * o_all[tq + half:]], axis=0)
        ms = jnp.mean(o * o, axis=-1, keepdims=True)
        y = o * lax.rsqrt(ms + RMS_EPS) * nw_ref[...] * out_scale
        o_ref[0, blk(qi), :] = y.astype(o_ref.dtype)
        m_ref[qi] = jnp.full((2 * tq, LANES), MASK_VALUE, F32)
        acc_ref[qi] = jnp.zeros_like(acc)

    def diag_pair(i, carry):
        diag_scores(jnp.minimum(2 * i + 1, nq - 1), s1_ref)
        diag_update(2 * i, s0_ref)
        diag_scores(jnp.minimum(2 * i + 2, nq - 1), s0_ref)
        diag_update(2 * i + 1, s1_ref)
        return carry

    diag_scores(0, s0_ref)
    lax.fori_loop(0, nq // 2, diag_pair, 0)


def _attn(proj3, bias_tiles, lam, norm_w, *, out_scale):
    B, S, _ = proj3.shape
    H = DIFF_HEADS
    block = ATT_BLOCK
    nq = S // block
    sched = _attn_schedule(nq)
    assert nq % 2 == 0 and sched.shape[1] % 2 == 0, "pipeline steps are unrolled in pairs"
    assert bias_tiles.shape[1] == 2 and _bias_reach() <= LANES <= block // 2
    q_blk = 4 * HGRN_HEADS
    k_blk = q_blk + H
    v_blk = k_blk + H
    seq_spec = lambda j: pl.BlockSpec((1, S, LANES), lambda b, h, j=j: (b, 0, j + h))
    rows = 2 * block
    scratch = [pltpu.VMEM((rows, block), F32), pltpu.VMEM((rows, block), F32),
               pltpu.VMEM((nq, rows, LANES), F32),
               pltpu.VMEM((nq, rows, 2 * LANES), F32)]
    scratch_bytes = rows * (2 * block * 4 + nq * 3 * LANES * 4)
    vmem = (4 * 2 * S * LANES * 2 + 2 * block * block * 4 + scratch_bytes
            + 5 * rows * block * 4 + (6 << 20))
    return pl.pallas_call(
        functools.partial(_attn_kernel, block=block, out_scale=out_scale),
        grid=(B, H),
        in_specs=[
            pl.BlockSpec(memory_space=pltpu.SMEM),
            seq_spec(q_blk), seq_spec(k_blk), seq_spec(v_blk),
            pl.BlockSpec((1, 1, block, block), lambda b, h: (h, 0, 0, 0)),
            pl.BlockSpec((1, 1, LANES, LANES), lambda b, h: (h, 1, 0, block // LANES - 1)),
            pl.BlockSpec((1, LANES), lambda b, h: (0, 0)),
            pl.BlockSpec((1, DIFF_DV), lambda b, h: (0, 0)),
        ],
        out_specs=pl.BlockSpec((1, S, LANES), lambda b, h: (b, 0, h)),
        out_shape=jax.ShapeDtypeStruct((B, S, DIFF_WIDTH), BF16),
        scratch_shapes=scratch,
        compiler_params=pltpu.CompilerParams(
            dimension_semantics=("arbitrary", "arbitrary"),
            vmem_limit_bytes=_vmem_limit(vmem)),
        name="attn",
    )(jnp.asarray(sched), proj3, proj3, proj3, bias_tiles, bias_tiles, lam, norm_w)


def _out_ffn_kernel(x_ref, oh_ref, od_ref, ada_ref, wo_ref, g1_ref, b1_ref,
                    wg_ref, wu_ref, wd_ref, g2_ref, b2_ref, o_ref, x1_ref, u_ref, *, alpha):
    gate_m = ada_ref[0, 2:3, :]
    shift_f = ada_ref[0, 3:4, :]
    scale_f = ada_ref[0, 4:5, :]
    gate_f = ada_ref[0, 5:6, :]
    hw = oh_ref.shape[1]
    sub = x_ref.shape[0] // FFN_SUBTILES
    n_chunks = wg_ref.shape[1] // FF_CHUNK

    def rows_of(t):
        return slice(t * sub, (t + 1) * sub)

    def prep(rows):
        mix = (jnp.dot(oh_ref[rows, :], wo_ref[:hw, :], preferred_element_type=F32)
               + jnp.dot(od_ref[rows, :], wo_ref[hw:, :], preferred_element_type=F32))
        x1 = (_layer_norm_rows(alpha * x_ref[rows, :] + (1.0 + gate_m) * mix) * g1_ref[...]
              + b1_ref[...])
        x1_ref[rows, :] = x1
        u_ref[rows, :] = (_layer_norm_rows(x1) * (1.0 + scale_f) + shift_f).astype(BF16)

    def swiglu(rows):
        u = u_ref[rows, :]
        y = jnp.zeros((sub, o_ref.shape[1]), F32)
        for c in range(n_chunks):
            cols = slice(c * FF_CHUNK, (c + 1) * FF_CHUNK)
            a = jnp.dot(u, wg_ref[:, cols], preferred_element_type=F32)
            b = jnp.dot(u, wu_ref[:, cols], preferred_element_type=F32)
            hid = (a * (1.0 / (1.0 + jnp.exp(-a))) * b).astype(BF16)
            y = y + jnp.dot(hid, wd_ref[cols, :], preferred_element_type=F32)
        return y

    def finish(rows, y):
        o_ref[rows, :] = (_layer_norm_rows(alpha * x1_ref[rows, :] + (1.0 + gate_f) * y)
                          * g2_ref[...] + b2_ref[...])

    prep(rows_of(0))
    for t in range(FFN_SUBTILES):
        if t + 1 < FFN_SUBTILES:
            prep(rows_of(t + 1))
        y = swiglu(rows_of(t))
        finish(rows_of(t), y)


def _out_ffn(x2, oh2, od2, ada3, w_out, ln1_g, ln1_b, w_gate, w_up, w_down, ln2_g, ln2_b,
             *, seq, alpha):
    N, D = x2.shape
    tm = ROW_TILE * FFN_SUBTILES
    sub = ROW_TILE
    per_batch = seq // tm
    row = lambda w: pl.BlockSpec((tm, w), lambda i: (i, 0))
    const = lambda shape: pl.BlockSpec(shape, lambda i: (0, 0), pipeline_mode=pl.Buffered(1))
    vec = pl.BlockSpec((1, D), lambda i: (0, 0))
    weights = (w_out.size + w_gate.size + w_up.size + w_down.size) * 2
    vmem = (weights + 2 * 2 * tm * D * 4 + 2 * 2 * tm * oh2.shape[1] * 2 + tm * D * (4 + 2)
            + 4 * sub * D * 4 + 4 * sub * FF_CHUNK * 4 + (4 << 20))
    return pl.pallas_call(
        functools.partial(_out_ffn_kernel, alpha=alpha),
        grid=(N // tm,),
        in_specs=[
            row(D), row(oh2.shape[1]), row(od2.shape[1]),
            pl.BlockSpec((1, ada3.shape[1], D), lambda i: (i // per_batch, 0, 0)),
            const(w_out.shape), vec, vec,
            const(w_gate.shape), const(w_up.shape), const(w_down.shape), vec, vec,
        ],
        out_specs=row(D),
        out_shape=jax.ShapeDtypeStruct((N, D), F32),
        scratch_shapes=[pltpu.VMEM((tm, D), F32), pltpu.VMEM((tm, D), BF16)],
        compiler_params=pltpu.CompilerParams(
            dimension_semantics=("arbitrary",), vmem_limit_bytes=_vmem_limit(vmem)),
        name="out_ffn",
    )(x2, oh2, od2, ada3, w_out, ln1_g, ln1_b, w_gate, w_up, w_down, ln2_g, ln2_b)


def kernel(x, c, w_ada, b_ada, w_in, lb_logits, hgrn_norm_w, lam_q1, lam_k1, lam_q2, lam_k2,
           diff_norm_w, rel_bias, w_out, ln1_g, ln1_b, w_gate, w_up, w_down, ln2_g, ln2_b):
    B, S, D = x.shape
    depth = w_ada.shape[0]
    assert S % (ROW_TILE * max(FFN_SUBTILES, PROJ_SUBTILES)) == 0
    assert S % HGRN_TILE == 0 and S % ATT_BLOCK == 0
    assert HGRN_TILE % (CHUNK * HGRN_GROUP) == 0 and w_in.shape[2] % PROJ_CHUNK == 0
    assert w_gate.shape[2] % FF_CHUNK == 0
    alpha = (2.0 * depth) ** 0.25
    bias_tiles = _bias_tiles(rel_bias, block=ATT_BLOCK, n_tiles=_num_near_tiles(ATT_BLOCK))
    col_scales = ((0, HGRN_WIDTH, HGRN_DK ** -0.5),
                  (HGRN_WIDTH, 2 * HGRN_WIDTH, 0.5),
                  (3 * HGRN_WIDTH, 4 * HGRN_WIDTH, 0.5),
                  (4 * HGRN_WIDTH, 4 * HGRN_WIDTH + DIFF_WIDTH, LOG2E * DIFF_DH ** -0.5))
    for l in range(depth):
        lam_init = 0.8 - 0.6 * math.exp(-0.3 * l)
        ada, lb, lam = _prep(c, w_ada[l], b_ada[l][None, :], lb_logits,
                             lam_q1[l][None, :], lam_k1[l][None, :],
                             lam_q2[l][None, :], lam_k2[l][None, :],
                             layer=l, lam_init=lam_init)
        ada3 = ada.reshape(B, 6, D)
        x2 = x.reshape(B * S, D)
        proj = _in_proj(x2, ada3, w_in[l].astype(BF16), seq=S, col_scales=col_scales)
        proj3 = proj.reshape(B, S, proj.shape[1])
        o_h = _hgrn(proj3, lb, hgrn_norm_w[l][None, :])
        o_d = _attn(proj3, bias_tiles, lam, diff_norm_w[l][None, :],
                    out_scale=1.0 - lam_init)
        out = _out_ffn(x2, o_h.reshape(B * S, -1), o_d.reshape(B * S, -1), ada3,
                       w_out[l].astype(BF16), ln1_g[l][None, :], ln1_b[l][None, :],
                       w_gate[l].astype(BF16), w_up[l].astype(BF16), w_down[l].astype(BF16),
                       ln2_g[l][None, :], ln2_b[l][None, :], seq=S, alpha=alpha)
        x = out.reshape(B, S, D)
    return x
```

```python
import functools
import math

import jax
import jax.numpy as jnp
import numpy as np
from jax import lax
from jax.experimental import pallas as pl
from jax.experimental.pallas import tpu as pltpu

HGRN_HEADS = 4
HGRN_DK = 128
HGRN_DV = 128
HGRN_WIDTH = HGRN_HEADS * HGRN_DV
DIFF_HEADS = 4
DIFF_DH = 64
DIFF_DV = 2 * DIFF_DH
DIFF_WIDTH = DIFF_HEADS * DIFF_DV
N_BUCKETS = 32
MAX_DISTANCE = 128
CHUNK = 64
LN_EPS = 1e-5
RMS_EPS = 1e-6
LOG2E = math.log2(math.e)

LANES = 128
SUBLANES = 8
V7X_VMEM_BYTES = 64 * 1024 * 1024

ROW_TILE = 512
HGRN_TILE = 1024
HGRN_GROUP = 2
ATT_BLOCK = 512
ATT_UNROLL = 14
FF_CHUNK = 256
FFN_SUBTILES = 2
PROJ_SUBTILES = 2
PROJ_CHUNK = 512
MASK_VALUE = -1e30

BF16 = jnp.bfloat16
F32 = jnp.float32


def _vmem_limit(block_bytes):
    return int(min(block_bytes, V7X_VMEM_BYTES - 4 * 1024 * 1024))


def _layer_norm_rows(x):
    mu = jnp.mean(x, axis=-1, keepdims=True)
    xc = x - mu
    var = jnp.mean(xc * xc, axis=-1, keepdims=True)
    return xc * lax.rsqrt(var + LN_EPS)


def _dot_nt(a, b):
    return lax.dot_general(a, b, (((1,), (1,)), ((), ())), preferred_element_type=F32)


def _dot_tn(a, b):
    return lax.dot_general(a, b, (((0,), (0,)), ((), ())), preferred_element_type=F32)


def _prep_kernel(c_ref, w_ref, b_ref, lbl_ref, q1_ref, k1_ref, q2_ref, k2_ref,
                 ada_ref, lb_ref, lam_ref, *, layer, lam_init):
    c = c_ref[...]
    sc = c * (1.0 / (1.0 + jnp.exp(-c)))
    ada_ref[...] = jnp.dot(sc.astype(BF16), w_ref[...].astype(BF16),
                           preferred_element_type=F32) + b_ref[...]

    @pl.when(pl.program_id(0) == 0)
    def _():
        n_rows = lbl_ref.shape[0]
        rows = [lbl_ref[r:r + 1, :] for r in range(n_rows)]
        mx = functools.reduce(jnp.maximum, rows)
        es = [jnp.exp(r - mx) for r in rows]
        den = functools.reduce(lambda a, b: a + b, es)
        num = functools.reduce(lambda a, b: a + b, es[:layer + 1])
        lb_ref[...] = num / den
        s1 = jnp.sum(q1_ref[...] * k1_ref[...], axis=-1, keepdims=True)
        s2 = jnp.sum(q2_ref[...] * k2_ref[...], axis=-1, keepdims=True)
        lam = jnp.exp(s1) - jnp.exp(s2) + lam_init
        lam_ref[...] = jnp.broadcast_to(lam, lam_ref.shape)


def _prep(c, w_ada, b_ada, lb_logits, lam_q1, lam_k1, lam_q2, lam_k2, *, layer, lam_init):
    B, D = c.shape
    n_out = w_ada.shape[1]
    col = D
    width = lb_logits.shape[1]
    small = lambda shape: pl.BlockSpec(shape, lambda j: (0,) * len(shape))
    return pl.pallas_call(
        functools.partial(_prep_kernel, layer=layer, lam_init=lam_init),
        grid=(n_out // col,),
        in_specs=[
            small((B, D)),
            pl.BlockSpec((D, col), lambda j: (0, j)),
            pl.BlockSpec((1, col), lambda j: (0, j)),
            small(lb_logits.shape),
            small((1, DIFF_DH)), small((1, DIFF_DH)), small((1, DIFF_DH)), small((1, DIFF_DH)),
        ],
        out_specs=[
            pl.BlockSpec((B, col), lambda j: (0, j)),
            small((1, width)),
            small((1, LANES)),
        ],
        out_shape=[
            jax.ShapeDtypeStruct((B, n_out), F32),
            jax.ShapeDtypeStruct((1, width), F32),
            jax.ShapeDtypeStruct((1, LANES), F32),
        ],
        compiler_params=pltpu.CompilerParams(
            dimension_semantics=("arbitrary",),
            vmem_limit_bytes=_vmem_limit(4 * D * col * 4 + (8 << 20))),
        name="prep",
    )(c, w_ada, b_ada, lb_logits, lam_q1, lam_k1, lam_q2, lam_k2)


def _t5_bucket_np(dist):
    max_exact = N_BUCKETS // 2
    d = np.maximum(dist, 1).astype(np.float32)
    large = max_exact + (np.log(d / max_exact) / math.log(MAX_DISTANCE / max_exact)
                         * (N_BUCKETS - max_exact)).astype(np.int32)
    large = np.minimum(large, N_BUCKETS - 1)
    return np.where(dist < max_exact, dist, large)


def _bucket_starts():
    buckets = _t5_bucket_np(np.arange(0, 64 * MAX_DISTANCE))
    assert np.all(np.diff(buckets) >= 0) and set(buckets) == set(range(N_BUCKETS))
    return [int(np.argmax(buckets == b)) for b in range(N_BUCKETS)]


def _bias_reach():
    return _bucket_starts()[-1]


def _num_near_tiles(block):
    t = 0
    while t * block - (block - 1) < _bias_reach():
        t += 1
    return t


def _bias_kernel(tab_ref, o_ref, *, block):
    h = pl.program_id(0)
    t = pl.program_id(1)
    r = lax.broadcasted_iota(jnp.int32, (block, block), 0)
    c = lax.broadcasted_iota(jnp.int32, (block, block), 1)
    dist = r - c + t * block
    bias = jnp.full((block, block), tab_ref[0, h], F32)
    for b, first in enumerate(_bucket_starts()):
        if b:
            bias = jnp.where(dist >= first, tab_ref[b, h], bias)
    shifted = (bias - tab_ref[N_BUCKETS - 1, h]) * LOG2E
    o_ref[0, 0] = jnp.where(dist >= 0, shifted, MASK_VALUE)


def _bias_tiles(rel_bias, *, block, n_tiles):
    H = rel_bias.shape[1]
    return pl.pallas_call(
        functools.partial(_bias_kernel, block=block),
        grid=(H, n_tiles),
        in_specs=[pl.BlockSpec(memory_space=pltpu.SMEM)],
        out_specs=pl.BlockSpec((1, 1, block, block), lambda h, t: (h, t, 0, 0)),
        out_shape=jax.ShapeDtypeStruct((H, n_tiles, block, block), F32),
        compiler_params=pltpu.CompilerParams(dimension_semantics=("arbitrary", "arbitrary")),
        name="bias_tiles",
    )(rel_bias)


def _in_proj_kernel(x_ref, ada_ref, w_ref, o_ref, *, col_scales):
    shift = ada_ref[0, 0:1, :]
    scale = ada_ref[0, 1:2, :]
    n_out = o_ref.shape[1]
    sub = x_ref.shape[0] // PROJ_SUBTILES

    def modulated(t):
        x = x_ref[t * sub:(t + 1) * sub, :]
        return (_layer_norm_rows(x) * (1.0 + scale) + shift).astype(BF16)

    u = modulated(0)
    for t in range(PROJ_SUBTILES):
        u_next = modulated(t + 1) if t + 1 < PROJ_SUBTILES else None
        for c0 in range(0, n_out, PROJ_CHUNK):
            acc = jnp.dot(u, w_ref[:, c0:c0 + PROJ_CHUNK], preferred_element_type=F32)
            for lo, hi, s in col_scales:
                if lo <= c0 < hi:
                    acc = acc * s
            o_ref[t * sub:(t + 1) * sub, c0:c0 + PROJ_CHUNK] = acc.astype(o_ref.dtype)
        u = u_next


def _in_proj(x2, ada3, w_in, *, seq, col_scales):
    N, D = x2.shape
    n_out = w_in.shape[1]
    tm = ROW_TILE * PROJ_SUBTILES
    per_batch = seq // tm
    assert all(lo % PROJ_CHUNK == 0 and hi % PROJ_CHUNK == 0 for lo, hi, _ in col_scales)
    vmem = (2 * tm * D * 4 + D * n_out * 2 + 2 * tm * n_out * 2 + 2 * 8 * D * 4
            + 3 * tm * D * 4 + 2 * tm * PROJ_CHUNK * 4 + (4 << 20))
    return pl.pallas_call(
        functools.partial(_in_proj_kernel, col_scales=col_scales),
        grid=(N // tm,),
        in_specs=[
            pl.BlockSpec((tm, D), lambda i: (i, 0)),
            pl.BlockSpec((1, ada3.shape[1], D), lambda i: (i // per_batch, 0, 0)),
            pl.BlockSpec((D, n_out), lambda i: (0, 0), pipeline_mode=pl.Buffered(1)),
        ],
        out_specs=pl.BlockSpec((tm, n_out), lambda i: (i, 0)),
        out_shape=jax.ShapeDtypeStruct((N, n_out), BF16),
        compiler_params=pltpu.CompilerParams(
            dimension_semantics=("arbitrary",), vmem_limit_bytes=_vmem_limit(vmem)),
        name="in_proj",
    )(x2, ada3, w_in)


def _decay_matrix():
    n = 2 * CHUNK + 2 * SUBLANES
    r = lax.broadcasted_iota(jnp.int32, (n, CHUNK), 0)
    s = lax.broadcasted_iota(jnp.int32, (n, CHUNK), 1)
    upto_mid = (s < CHUNK // 2).astype(F32)
    d1 = (s <= r).astype(F32) - upto_mid
    d2 = (s > r - CHUNK).astype(F32)
    return jnp.where(r < CHUNK, d1, jnp.where(r < 2 * CHUNK, d2,
                     jnp.where(r < 2 * CHUNK + SUBLANES, upto_mid, 1.0))).astype(BF16)


def _split3(x):
    hi = x.astype(BF16)
    r1 = x - hi.astype(F32)
    mid = r1.astype(BF16)
    lo = (r1 - mid.astype(F32)).astype(BF16)
    return hi, mid, lo


def _hgrn_kernel(q_ref, f_ref, i_ref, g_ref, lb_ref, nw_ref, o_ref,
                 st_ref, qd_ref, kd_ref, qin_ref, kst_ref, dl_ref, of_ref):
    @pl.when(pl.program_id(1) == 0)
    def _():
        st_ref[...] = jnp.zeros_like(st_ref)

    lb = lb_ref[...]
    one_m_lb = 1.0 - lb
    nw = nw_ref[...]
    n_chunks = q_ref.shape[1] // CHUNK
    tri_r = lax.broadcasted_iota(jnp.int32, (CHUNK, CHUNK), 0)
    tri_c = lax.broadcasted_iota(jnp.int32, (CHUNK, CHUNK), 1)
    causal = tri_r >= tri_c
    decay = _decay_matrix()
    decay3 = jnp.concatenate([decay, decay, decay], axis=1)
    sub = CHUNK // SUBLANES

    half_span = 0.5 * one_m_lb
    f_mid = lb + half_span

    def chunk_rows(n):
        return slice(n * CHUNK, (n + 1) * CHUNK)

    def decay_rows(n):
        return slice(n * SUBLANES, (n + 1) * SUBLANES)

    def heads():
        return [slice(h * HGRN_DK, (h + 1) * HGRN_DK) for h in range(HGRN_HEADS)]


    def gates(n):
        x = f_ref[0, chunk_rows(n), :].astype(F32)
        th = half_span * jnp.tanh(x)
        lf2 = jnp.log2(f_mid + th)
        return half_span - th, jnp.concatenate(_split3(lf2), axis=0)

    def operands(n, kc, rel):
        rows = chunk_rows(n)
        q = q_ref[0, rows, :].astype(F32)
        d_mid = rel[:CHUNK]
        e1 = jnp.exp2(d_mid)
        qd_f = q * e1
        qd_ref[rows, :] = qd_f.astype(BF16)
        kd_ref[rows, :] = (kc * jnp.exp2(-d_mid)).astype(BF16)
        e_mid = jnp.exp2(rel[2 * CHUNK:2 * CHUNK + SUBLANES])
        qin_ref[rows, :] = (qd_f.reshape(sub, SUBLANES, -1) * e_mid[None]).reshape(qd_f.shape).astype(BF16)
        kst_ref[rows, :] = (kc * jnp.exp2(rel[CHUNK:2 * CHUNK])).astype(BF16)
        dl_ref[decay_rows(n), :] = jnp.exp2(rel[2 * CHUNK + SUBLANES:])

    def prep_group(ns):
        g1 = [gates(n) for n in ns]
        rels = [jnp.dot(decay3, parts, preferred_element_type=F32) for _, parts in g1]
        for n, (kc, _), rel in zip(ns, g1, rels):
            operands(n, kc, rel)

    def mix_group(ns):
        local = []
        for n in ns:
            rows = chunk_rows(n)
            v = i_ref[0, rows, :]
            per_head = []
            for sl in heads():
                a = jnp.where(causal, _dot_nt(qd_ref[rows, sl], kd_ref[rows, sl]), 0.0)
                per_head.append((a.astype(BF16), _dot_tn(v[:, sl], kst_ref[rows, sl])))
            local.append((rows, v, per_head))
        for n, (rows, v, per_head) in zip(ns, local):
            d_last = dl_ref[decay_rows(n), :]
            for h, (sl, (a, upd)) in enumerate(zip(heads(), per_head)):
                st = st_ref[h]
                of_ref[rows, sl] = (jnp.dot(a, v[:, sl], preferred_element_type=F32)
                                    + _dot_nt(qin_ref[rows, sl], st.astype(BF16)))
                st_new = st.reshape(HGRN_DV // SUBLANES, SUBLANES, HGRN_DK) * d_last[None, :, sl]
                st_ref[h] = st_new.reshape(HGRN_DV, HGRN_DK) + upd

    def norm_block(n):
        rows = chunk_rows(n)
        hg = g_ref[0, rows, :].astype(F32)
        gate = hg + hg * jnp.tanh(hg)
        for sl in heads():
            o = of_ref[rows, sl]
            ms = jnp.mean(o * o, axis=-1, keepdims=True)
            o_ref[0, rows, sl] = (o * lax.rsqrt(ms + RMS_EPS) * nw * gate[:, sl]).astype(o_ref.dtype)

    groups = [list(range(g * HGRN_GROUP, (g + 1) * HGRN_GROUP)) for g in range(n_chunks // HGRN_GROUP)]
    for step in range(len(groups) + 2):
        if step < len(groups):
            prep_group(groups[step])
        if 0 <= step - 1 < len(groups):
            mix_group(groups[step - 1])
        if 0 <= step - 2 < len(groups):
            for n in groups[step - 2]:
                norm_block(n)


def _hgrn(proj3, lb, norm_w):
    B, S, _ = proj3.shape
    ts = HGRN_TILE
    W = HGRN_WIDTH
    spec = lambda j: pl.BlockSpec((1, ts, W), lambda b, s, j=j: (b, s, j))
    vmem = 5 * 2 * ts * W * 2 + HGRN_HEADS * HGRN_DK * HGRN_DV * 4 + (16 << 20)
    return pl.pallas_call(
        _hgrn_kernel,
        grid=(B, S // ts),
        in_specs=[spec(0), spec(1), spec(2), spec(3),
                  pl.BlockSpec((1, W), lambda b, s: (0, 0)),
                  pl.BlockSpec((1, HGRN_DV), lambda b, s: (0, 0))],
        out_specs=pl.BlockSpec((1, ts, W), lambda b, s: (b, s, 0)),
        out_shape=jax.ShapeDtypeStruct((B, S, W), BF16),
        scratch_shapes=[pltpu.VMEM((HGRN_HEADS, HGRN_DV, HGRN_DK), F32),
                        pltpu.VMEM((ts, W), BF16), pltpu.VMEM((ts, W), BF16),
                        pltpu.VMEM((ts, W), BF16), pltpu.VMEM((ts, W), BF16),
                        pltpu.VMEM((ts // CHUNK * SUBLANES, W), F32),
                        pltpu.VMEM((ts, W), F32)],
        compiler_params=pltpu.CompilerParams(
            dimension_semantics=("arbitrary", "arbitrary"), vmem_limit_bytes=_vmem_limit(vmem)),
        name="hgrn",
    )(proj3, proj3, proj3, proj3, lb, norm_w)


def _attn_schedule(nq):
    pairs = [(qi, kj) for qi in range(1, nq) for kj in range(qi)]
    nxt = pairs[1:] + pairs[-1:]
    return np.array([[p[0] for p in nxt], [p[1] for p in nxt],
                     [p[0] for p in pairs], [p[1] for p in pairs]], np.int32)


def _attn_kernel(sched_ref, q_ref, k_ref, v_ref, diag_ref, near_ref, lam_ref, nw_ref, o_ref,
                 s0_ref, s1_ref, m_ref, acc_ref, *, block, out_scale):
    tq = block
    half = tq // 2
    nq = q_ref.shape[1] // tq
    n_off = sched_ref.shape[1]
    ones = jnp.ones((block, LANES), BF16)
    lam = lam_ref[...]

    def blk(j):
        return pl.ds(pl.multiple_of(j * block, block), block)

    def stacked_queries(qi):
        q12 = q_ref[0, blk(qi), :]
        lane = lax.broadcasted_iota(jnp.int32, q12.shape, 1)
        zero = jnp.zeros_like(q12)
        q1 = jnp.where(lane < DIFF_DH, q12, zero)
        q2 = jnp.where(lane >= DIFF_DH, q12, zero)
        return jnp.concatenate([q1[:half], q2[:half], q1[half:], q2[half:]], axis=0)

    def values(kj):
        return jnp.concatenate([v_ref[0, blk(kj), :], ones], axis=1)

    def online_softmax(s, rows, qi, v_ext):
        n_lane_tiles = s.shape[1] // LANES
        mx = s[:, :LANES]
        for c in range(1, n_lane_tiles):
            mx = jnp.maximum(mx, s[:, c * LANES:(c + 1) * LANES])
        m_prev = m_ref[qi, rows, :]
        m_new = jnp.maximum(m_prev, jnp.max(mx, axis=-1, keepdims=True))
        alpha = jnp.exp2(m_prev - m_new)
        p = jnp.exp2(s - jnp.concatenate([m_new] * n_lane_tiles, axis=1)).astype(BF16)
        pv = jnp.dot(p, v_ext, preferred_element_type=F32)
        acc_ref[qi, rows, :] = jnp.concatenate([alpha, alpha], axis=1) * acc_ref[qi, rows, :] + pv
        m_ref[qi, rows, :] = m_new

    @pl.when((pl.program_id(0) == 0) & (pl.program_id(1) == 0))
    def _():
        m_ref[...] = jnp.full_like(m_ref, MASK_VALUE)
        acc_ref[...] = jnp.zeros_like(acc_ref)

    def off_scores(qi, kj, s_ref):
        s_ref[...] = _dot_nt(stacked_queries(qi), k_ref[0, blk(kj), :])

    def off_update(step, s_ref):
        qi = sched_ref[2, step]
        kj = sched_ref[3, step]
        corner = jnp.where(qi - kj == 1, 1.0, 0.0) * near_ref[0, 0]
        for r0 in (0, half):
            s_ref[r0:r0 + LANES, tq - LANES:] += corner
        online_softmax(s_ref[...], slice(0, 2 * tq), qi, values(kj))

    unroll = max(u for u in range(2, ATT_UNROLL + 1, 2) if n_off % u == 0)

    def off_steps(i, carry):
        for j in range(0, unroll, 2):
            step = unroll * i + j
            off_scores(sched_ref[0, step], sched_ref[1, step], s1_ref)
            off_update(step, s0_ref)
            off_scores(sched_ref[0, step + 1], sched_ref[1, step + 1], s0_ref)
            off_update(step + 1, s1_ref)
        return carry

    off_scores(1, 0, s0_ref)
    lax.fori_loop(0, n_off // unroll, off_steps, 0)

    def diag_scores(qi, s_ref):
        qz = stacked_queries(qi)
        s_ref[:tq, :half] = _dot_nt(qz[:tq], k_ref[0, pl.ds(pl.multiple_of(qi * tq, tq), half), :])
        s_ref[tq:, :] = _dot_nt(qz[tq:], k_ref[0, blk(qi), :])

    def diag_update(qi, s_ref):
        bias = diag_ref[0, 0]
        v_ext = values(qi)
        bias_a = bias[:half, :half]
        bias_b = bias[half:, :]
        for r0 in (0, half):
            s_ref[r0:r0 + half, :half] += bias_a
            s_ref[tq + r0:tq + r0 + half, :] += bias_b
        online_softmax(s_ref[:tq, :half], slice(0, tq), qi, v_ext[:half])
        online_softmax(s_ref[tq:, :], slice(tq, 2 * tq), qi, v_ext)
        acc = acc_ref[qi]
        o_all = acc[:, :DIFF_DV] / acc[:, DIFF_DV:]
        o = jnp.concatenate([o_all[:half] - lam * o_all[half:tq],
                             o_all[tq:tq + half] - lam * o_all[tq + half:]], axis=0)
        ms = jnp.mean(o * o, axis=-1, keepdims=True)
        y = o * lax.rsqrt(ms + RMS_EPS) * nw_ref[...] * out_scale
        o_ref[0, blk(qi), :] = y.astype(o_ref.dtype)
        m_ref[qi] = jnp.full((2 * tq, LANES), MASK_VALUE, F32)
        acc_ref[qi] = jnp.zeros_like(acc)

    def diag_pair(i, carry):
        diag_scores(jnp.minimum(2 * i + 1, nq - 1), s1_ref)
        diag_update(2 * i, s0_ref)
        diag_scores(jnp.minimum(2 * i + 2, nq - 1), s0_ref)
        diag_update(2 * i + 1, s1_ref)
        return carry

    diag_scores(0, s0_ref)
    lax.fori_loop(0, nq // 2, diag_pair, 0)


def _attn(proj3, bias_tiles, lam, norm_w, *, out_scale):
    B, S, _ = proj3.shape
    H = DIFF_HEADS
    block = ATT_BLOCK
    nq = S // block
    sched = _attn_schedule(nq)
    assert nq % 2 == 0 and sched.shape[1] % 2 == 0, "pipeline steps are unrolled in pairs"
    assert bias_tiles.shape[1] == 2 and _bias_reach() <= LANES <= block // 2
    q_blk = 4 * HGRN_HEADS
    k_blk = q_blk + H
    v_blk = k_blk + H
    seq_spec = lambda j: pl.BlockSpec((1, S, LANES), lambda b, h, j=j: (b, 0, j + h))
    rows = 2 * block
    scratch = [pltpu.VMEM((rows, block), F32), pltpu.VMEM((rows, block), F32),
               pltpu.VMEM((nq, rows, LANES), F32),
               pltpu.VMEM((nq, rows, 2 * LANES), F32)]
    scratch_bytes = rows * (2 * block * 4 + nq * 3 * LANES * 4)
    vmem = (4 * 2 * S * LANES * 2 + 2 * block * block * 4 + scratch_bytes
            + 5 * rows * block * 4 + (6 << 20))
    return pl.pallas_call(
        functools.partial(_attn_kernel, block=block, out_scale=out_scale),
        grid=(B, H),
        in_specs=[
            pl.BlockSpec(memory_space=pltpu.SMEM),
            seq_spec(q_blk), seq_spec(k_blk), seq_spec(v_blk),
            pl.BlockSpec((1, 1, block, block), lambda b, h: (h, 0, 0, 0)),
            pl.BlockSpec((1, 1, LANES, LANES), lambda b, h: (h, 1, 0, block // LANES - 1)),
            pl.BlockSpec((1, LANES), lambda b, h: (0, 0)),
            pl.BlockSpec((1, DIFF_DV), lambda b, h: (0, 0)),
        ],
        out_specs=pl.BlockSpec((1, S, LANES), lambda b, h: (b, 0, h)),
        out_shape=jax.ShapeDtypeStruct((B, S, DIFF_WIDTH), BF16),
        scratch_shapes=scratch,
        compiler_params=pltpu.CompilerParams(
            dimension_semantics=("arbitrary", "arbitrary"),
            vmem_limit_bytes=_vmem_limit(vmem)),
        name="attn",
    )(jnp.asarray(sched), proj3, proj3, proj3, bias_tiles, bias_tiles, lam, norm_w)


def _out_ffn_kernel(x_ref, oh_ref, od_ref, ada_ref, wo_ref, g1_ref, b1_ref,
                    wg_ref, wu_ref, wd_ref, g2_ref, b2_ref, o_ref, x1_ref, u_ref, *, alpha):
    gate_m = ada_ref[0, 2:3, :]
    shift_f = ada_ref[0, 3:4, :]
    scale_f = ada_ref[0, 4:5, :]
    gate_f = ada_ref[0, 5:6, :]
    hw = oh_ref.shape[1]
    sub = x_ref.shape[0] // FFN_SUBTILES
    n_chunks = wg_ref.shape[1] // FF_CHUNK

    def rows_of(t):
        return slice(t * sub, (t + 1) * sub)

    def prep(rows):
        mix = (jnp.dot(oh_ref[rows, :], wo_ref[:hw, :], preferred_element_type=F32)
               + jnp.dot(od_ref[rows, :], wo_ref[hw:, :], preferred_element_type=F32))
        x1 = (_layer_norm_rows(alpha * x_ref[rows, :] + (1.0 + gate_m) * mix) * g1_ref[...]
              + b1_ref[...])
        x1_ref[rows, :] = x1
        u_ref[rows, :] = (_layer_norm_rows(x1) * (1.0 + scale_f) + shift_f).astype(BF16)

    def swiglu(rows):
        u = u_ref[rows, :]
        y = jnp.zeros((sub, o_ref.shape[1]), F32)
        for c in range(n_chunks):
            cols = slice(c * FF_CHUNK, (c + 1) * FF_CHUNK)
            a = jnp.dot(u, wg_ref[:, cols], preferred_element_type=F32)
            b = jnp.dot(u, wu_ref[:, cols], preferred_element_type=F32)
            hid = (a * (1.0 / (1.0 + jnp.exp(-a))) * b).astype(BF16)
            y = y + jnp.dot(hid, wd_ref[cols, :], preferred_element_type=F32)
        return y

    def finish(rows, y):
        o_ref[rows, :] = (_layer_norm_rows(alpha * x1_ref[rows, :] + (1.0 + gate_f) * y)
                          * g2_ref[...] + b2_ref[...])

    prep(rows_of(0))
    for t in range(FFN_SUBTILES):
        if t + 1 < FFN_SUBTILES:
            prep(rows_of(t + 1))
        y = swiglu(rows_of(t))
        finish(rows_of(t), y)


def _out_ffn(x2, oh2, od2, ada3, w_out, ln1_g, ln1_b, w_gate, w_up, w_down, ln2_g, ln2_b,
             *, seq, alpha):
    N, D = x2.shape
    tm = ROW_TILE * FFN_SUBTILES
    sub = ROW_TILE
    per_batch = seq // tm
    row = lambda w: pl.BlockSpec((tm, w), lambda i: (i, 0))
    const = lambda shape: pl.BlockSpec(shape, lambda i: (0, 0), pipeline_mode=pl.Buffered(1))
    vec = pl.BlockSpec((1, D), lambda i: (0, 0))
    weights = (w_out.size + w_gate.size + w_up.size + w_down.size) * 2
    vmem = (weights + 2 * 2 * tm * D * 4 + 2 * 2 * tm * oh2.shape[1] * 2 + tm * D * (4 + 2)
            + 4 * sub * D * 4 + 4 * sub * FF_CHUNK * 4 + (4 << 20))
    return pl.pallas_call(
        functools.partial(_out_ffn_kernel, alpha=alpha),
        grid=(N // tm,),
        in_specs=[
            row(D), row(oh2.shape[1]), row(od2.shape[1]),
            pl.BlockSpec((1, ada3.shape[1], D), lambda i: (i // per_batch, 0, 0)),
            const(w_out.shape), vec, vec,
            const(w_gate.shape), const(w_up.shape), const(w_down.shape), vec, vec,
        ],
        out_specs=row(D),
        out_shape=jax.ShapeDtypeStruct((N, D), F32),
        scratch_shapes=[pltpu.VMEM((tm, D), F32), pltpu.VMEM((tm, D), BF16)],
        compiler_params=pltpu.CompilerParams(
            dimension_semantics=("arbitrary",), vmem_limit_bytes=_vmem_limit(vmem)),
        name="out_ffn",
    )(x2, oh2, od2, ada3, w_out, ln1_g, ln1_b, w_gate, w_up, w_down, ln2_g, ln2_b)


def kernel(x, c, w_ada, b_ada, w_in, lb_logits, hgrn_norm_w, lam_q1, lam_k1, lam_q2, lam_k2,
           diff_norm_w, rel_bias, w_out, ln1_g, ln1_b, w_gate, w_up, w_down, ln2_g, ln2_b):
    B, S, D = x.shape
    depth = w_ada.shape[0]
    assert S % (ROW_TILE * max(FFN_SUBTILES, PROJ_SUBTILES)) == 0
    assert S % HGRN_TILE == 0 and S % ATT_BLOCK == 0
    assert HGRN_TILE % (CHUNK * HGRN_GROUP) == 0 and w_in.shape[2] % PROJ_CHUNK == 0
    assert w_gate.shape[2] % FF_CHUNK == 0
    alpha = (2.0 * depth) ** 0.25
    bias_tiles = _bias_tiles(rel_bias, block=ATT_BLOCK, n_tiles=_num_near_tiles(ATT_BLOCK))
    col_scales = ((0, HGRN_WIDTH, HGRN_DK ** -0.5),
                  (HGRN_WIDTH, 2 * HGRN_WIDTH, 0.5),
                  (3 * HGRN_WIDTH, 4 * HGRN_WIDTH, 0.5),
                  (4 * HGRN_WIDTH, 4 * HGRN_WIDTH + DIFF_WIDTH, LOG2E * DIFF_DH ** -0.5))
    for l in range(depth):
        lam_init = 0.8 - 0.6 * math.exp(-0.3 * l)
        ada, lb, lam = _prep(c, w_ada[l], b_ada[l][None, :], lb_logits,
                             lam_q1[l][None, :], lam_k1[l][None, :],
                             lam_q2[l][None, :], lam_k2[l][None, :],
                             layer=l, lam_init=lam_init)
        ada3 = ada.reshape(B, 6, D)
        x2 = x.reshape(B * S, D)
        proj = _in_proj(x2, ada3, w_in[l].astype(BF16), seq=S, col_scales=col_scales)
        proj3 = proj.reshape(B, S, proj.shape[1])
        o_h = _hgrn(proj3, lb, hgrn_norm_w[l][None, :])
        o_d = _attn(proj3, bias_tiles, lam, diff_norm_w[l][None, :],
                    out_scale=1.0 - lam_init)
        out = _out_ffn(x2, o_h.reshape(B * S, -1), o_d.reshape(B * S, -1), ada3,
                       w_out[l].astype(BF16), ln1_g[l][None, :], ln1_b[l][None, :],
                       w_gate[l].astype(BF16), w_up[l].astype(BF16), w_down[l].astype(BF16),
                       ln2_g[l][None, :], ln2_b[l][None, :], seq=S, alpha=alpha)
        x = out.reshape(B, S, D)
    return x
```

```python
import functools
import math

import jax
import jax.numpy as jnp
import numpy as np
from jax import lax
from jax.experimental import pallas as pl
from jax.experimental.pallas import tpu as pltpu

HGRN_HEADS = 4
HGRN_DK = 128
HGRN_DV = 128
HGRN_WIDTH = HGRN_HEADS * HGRN_DV
DIFF_HEADS = 4
DIFF_DH = 64
DIFF_DV = 2 * DIFF_DH
DIFF_WIDTH = DIFF_HEADS * DIFF_DV
N_BUCKETS = 32
MAX_DISTANCE = 128
CHUNK = 64
LN_EPS = 1e-5
RMS_EPS = 1e-6
LOG2E = math.log2(math.e)

LANES = 128
SUBLANES = 8
V7X_VMEM_BYTES = 64 * 1024 * 1024

ROW_TILE = 512
HGRN_TILE = 1024
HGRN_GROUP = 2
ATT_BLOCK = 512
ATT_UNROLL = 14
FF_CHUNK = 256
FFN_SUBTILES = 2
PROJ_SUBTILES = 2
PROJ_CHUNK = 512
MASK_VALUE = -1e30

BF16 = jnp.bfloat16
F32 = jnp.float32


def _vmem_limit(block_bytes):
    return int(min(block_bytes, V7X_VMEM_BYTES - 4 * 1024 * 1024))


def _layer_norm_rows(x):
    mu = jnp.mean(x, axis=-1, keepdims=True)
    xc = x - mu
    var = jnp.mean(xc * xc, axis=-1, keepdims=True)
    return xc * lax.rsqrt(var + LN_EPS)


def _dot_nt(a, b):
    return lax.dot_general(a, b, (((1,), (1,)), ((), ())), preferred_element_type=F32)


def _dot_tn(a, b):
    return lax.dot_general(a, b, (((0,), (0,)), ((), ())), preferred_element_type=F32)


def _prep_kernel(c_ref, w_ref, b_ref, lbl_ref, q1_ref, k1_ref, q2_ref, k2_ref,
                 ada_ref, lb_ref, lam_ref, *, layer, lam_init):
    c = c_ref[...]
    sc = c * (1.0 / (1.0 + jnp.exp(-c)))
    ada_ref[...] = jnp.dot(sc.astype(BF16), w_ref[...].astype(BF16),
                           preferred_element_type=F32) + b_ref[...]

    @pl.when(pl.program_id(0) == 0)
    def _():
        n_rows = lbl_ref.shape[0]
        rows = [lbl_ref[r:r + 1, :] for r in range(n_rows)]
        mx = functools.reduce(jnp.maximum, rows)
        es = [jnp.exp(r - mx) for r in rows]
        den = functools.reduce(lambda a, b: a + b, es)
        num = functools.reduce(lambda a, b: a + b, es[:layer + 1])
        lb_ref[...] = num / den
        s1 = jnp.sum(q1_ref[...] * k1_ref[...], axis=-1, keepdims=True)
        s2 = jnp.sum(q2_ref[...] * k2_ref[...], axis=-1, keepdims=True)
        lam = jnp.exp(s1) - jnp.exp(s2) + lam_init
        lam_ref[...] = jnp.broadcast_to(lam, lam_ref.shape)


def _prep(c, w_ada, b_ada, lb_logits, lam_q1, lam_k1, lam_q2, lam_k2, *, layer, lam_init):
    B, D = c.shape
    n_out = w_ada.shape[1]
    col = D
    width = lb_logits.shape[1]
    small = lambda shape: pl.BlockSpec(shape, lambda j: (0,) * len(shape))
    return pl.pallas_call(
        functools.partial(_prep_kernel, layer=layer, lam_init=lam_init),
        grid=(n_out // col,),
        in_specs=[
            small((B, D)),
            pl.BlockSpec((D, col), lambda j: (0, j)),
            pl.BlockSpec((1, col), lambda j: (0, j)),
            small(lb_logits.shape),
            small((1, DIFF_DH)), small((1, DIFF_DH)), small((1, DIFF_DH)), small((1, DIFF_DH)),
        ],
        out_specs=[
            pl.BlockSpec((B, col), lambda j: (0, j)),
            small((1, width)),
            small((1, LANES)),
        ],
        out_shape=[
            jax.ShapeDtypeStruct((B, n_out), F32),
            jax.ShapeDtypeStruct((1, width), F32),
            jax.ShapeDtypeStruct((1, LANES), F32),
        ],
        compiler_params=pltpu.CompilerParams(
            dimension_semantics=("arbitrary",),
            vmem_limit_bytes=_vmem_limit(4 * D * col * 4 + (8 << 20))),
        name="prep",
    )(c, w_ada, b_ada, lb_logits, lam_q1, lam_k1, lam_q2, lam_k2)


def _t5_bucket_np(dist):
    max_exact = N_BUCKETS // 2
    d = np.maximum(dist, 1).astype(np.float32)
    large = max_exact + (np.log(d / max_exact) / math.log(MAX_DISTANCE / max_exact)
                         * (N_BUCKETS - max_exact)).astype(np.int32)
    large = np.minimum(large, N_BUCKETS - 1)
    return np.where(dist < max_exact, dist, large)


def _bucket_starts():
    buckets = _t5_bucket_np(np.arange(0, 64 * MAX_DISTANCE))
    assert np.all(np.diff(buckets) >= 0) and set(buckets) == set(range(N_BUCKETS))
    return [int(np.argmax(buckets == b)) for b in range(N_BUCKETS)]


def _bias_reach():
    return _bucket_starts()[-1]


def _num_near_tiles(block):
    t = 0
    while t * block - (block - 1) < _bias_reach():
        t += 1
    return t


def _bias_kernel(tab_ref, o_ref, *, block):
    h = pl.program_id(0)
    t = pl.program_id(1)
    r = lax.broadcasted_iota(jnp.int32, (block, block), 0)
    c = lax.broadcasted_iota(jnp.int32, (block, block), 1)
    dist = r - c + t * block
    bias = jnp.full((block, block), tab_ref[0, h], F32)
    for b, first in enumerate(_bucket_starts()):
        if b:
            bias = jnp.where(dist >= first, tab_ref[b, h], bias)
    shifted = (bias - tab_ref[N_BUCKETS - 1, h]) * LOG2E
    o_ref[0, 0] = jnp.where(dist >= 0, shifted, MASK_VALUE)


def _bias_tiles(rel_bias, *, block, n_tiles):
    H = rel_bias.shape[1]
    return pl.pallas_call(
        functools.partial(_bias_kernel, block=block),
        grid=(H, n_tiles),
        in_specs=[pl.BlockSpec(memory_space=pltpu.SMEM)],
        out_specs=pl.BlockSpec((1, 1, block, block), lambda h, t: (h, t, 0, 0)),
        out_shape=jax.ShapeDtypeStruct((H, n_tiles, block, block), F32),
        compiler_params=pltpu.CompilerParams(dimension_semantics=("arbitrary", "arbitrary")),
        name="bias_tiles",
    )(rel_bias)


def _in_proj_kernel(x_ref, ada_ref, w_ref, o_ref, *, col_scales):
    shift = ada_ref[0, 0:1, :]
    scale = ada_ref[0, 1:2, :]
    n_out = o_ref.shape[1]
    sub = x_ref.shape[0] // PROJ_SUBTILES

    def modulated(t):
        x = x_ref[t * sub:(t + 1) * sub, :]
        return (_layer_norm_rows(x) * (1.0 + scale) + shift).astype(BF16)

    u = modulated(0)
    for t in range(PROJ_SUBTILES):
        u_next = modulated(t + 1) if t + 1 < PROJ_SUBTILES else None
        for c0 in range(0, n_out, PROJ_CHUNK):
            acc = jnp.dot(u, w_ref[:, c0:c0 + PROJ_CHUNK], preferred_element_type=F32)
            for lo, hi, s in col_scales:
                if lo <= c0 < hi:
                    acc = acc * s
            o_ref[t * sub:(t + 1) * sub, c0:c0 + PROJ_CHUNK] = acc.astype(o_ref.dtype)
        u = u_next


def _in_proj(x2, ada3, w_in, *, seq, col_scales):
    N, D = x2.shape
    n_out = w_in.shape[1]
    tm = ROW_TILE * PROJ_SUBTILES
    per_batch = seq // tm
    assert all(lo % PROJ_CHUNK == 0 and hi % PROJ_CHUNK == 0 for lo, hi, _ in col_scales)
    vmem = (2 * tm * D * 4 + D * n_out * 2 + 2 * tm * n_out * 2 + 2 * 8 * D * 4
            + 3 * tm * D * 4 + 2 * tm * PROJ_CHUNK * 4 + (4 << 20))
    return pl.pallas_call(
        functools.partial(_in_proj_kernel, col_scales=col_scales),
        grid=(N // tm,),
        in_specs=[
            pl.BlockSpec((tm, D), lambda i: (i, 0)),
            pl.BlockSpec((1, ada3.shape[1], D), lambda i: (i // per_batch, 0, 0)),
            pl.BlockSpec((D, n_out), lambda i: (0, 0), pipeline_mode=pl.Buffered(1)),
        ],
        out_specs=pl.BlockSpec((tm, n_out), lambda i: (i, 0)),
        out_shape=jax.ShapeDtypeStruct((N, n_out), BF16),
        compiler_params=pltpu.CompilerParams(
            dimension_semantics=("arbitrary",), vmem_limit_bytes=_vmem_limit(vmem)),
        name="in_proj",
    )(x2, ada3, w_in)


def _decay_matrix():
    n = 2 * CHUNK + 2 * SUBLANES
    r = lax.broadcasted_iota(jnp.int32, (n, CHUNK), 0)
    s = lax.broadcasted_iota(jnp.int32, (n, CHUNK), 1)
    upto_mid = (s < CHUNK // 2).astype(F32)
    d1 = (s <= r).astype(F32) - upto_mid
    d2 = (s > r - CHUNK).astype(F32)
    return jnp.where(r < CHUNK, d1, jnp.where(r < 2 * CHUNK, d2,
                     jnp.where(r < 2 * CHUNK + SUBLANES, upto_mid, 1.0))).astype(BF16)


def _split3(x):
    hi = x.astype(BF16)
    r1 = x - hi.astype(F32)
    mid = r1.astype(BF16)
    lo = (r1 - mid.astype(F32)).astype(BF16)
    return hi, mid, lo


def _hgrn_kernel(q_ref, f_ref, i_ref, g_ref, lb_ref, nw_ref, o_ref,
                 st_ref, qd_ref, kd_ref, qin_ref, kst_ref, dl_ref, of_ref):
    @pl.when(pl.program_id(1) == 0)
    def _():
        st_ref[...] = jnp.zeros_like(st_ref)

    lb = lb_ref[...]
    one_m_lb = 1.0 - lb
    nw = nw_ref[...]
    n_chunks = q_ref.shape[1] // CHUNK
    tri_r = lax.broadcasted_iota(jnp.int32, (CHUNK, CHUNK), 0)
    tri_c = lax.broadcasted_iota(jnp.int32, (CHUNK, CHUNK), 1)
    causal = tri_r >= tri_c
    decay = _decay_matrix()
    decay3 = jnp.concatenate([decay, decay, decay], axis=1)
    sub = CHUNK // SUBLANES

    half_span = 0.5 * one_m_lb
    f_mid = lb + half_span

    def chunk_rows(n):
        return slice(n * CHUNK, (n + 1) * CHUNK)

    def decay_rows(n):
        return slice(n * SUBLANES, (n + 1) * SUBLANES)

    def heads():
        return [slice(h * HGRN_DK, (h + 1) * HGRN_DK) for h in range(HGRN_HEADS)]


    def gates(n):
        x = f_ref[0, chunk_rows(n), :].astype(F32)
        th = half_span * jnp.tanh(x)
        lf2 = jnp.log2(f_mid + th)
        return half_span - th, jnp.concatenate(_split3(lf2), axis=0)

    def operands(n, kc, rel):
        rows = chunk_rows(n)
        q = q_ref[0, rows, :].astype(F32)
        d_mid = rel[:CHUNK]
        e1 = jnp.exp2(d_mid)
        qd_f = q * e1
        qd_ref[rows, :] = qd_f.astype(BF16)
        kd_ref[rows, :] = (kc * jnp.exp2(-d_mid)).astype(BF16)
        e_mid = jnp.exp2(rel[2 * CHUNK:2 * CHUNK + SUBLANES])
        qin_ref[rows, :] = (qd_f.reshape(sub, SUBLANES, -1) * e_mid[None]).reshape(qd_f.shape).astype(BF16)
        kst_ref[rows, :] = (kc * jnp.exp2(rel[CHUNK:2 * CHUNK])).astype(BF16)
        dl_ref[decay_rows(n), :] = jnp.exp2(rel[2 * CHUNK + SUBLANES:])

    def prep_group(ns):
        g1 = [gates(n) for n in ns]
        rels = [jnp.dot(decay3, parts, preferred_element_type=F32) for _, parts in g1]
        for n, (kc, _), rel in zip(ns, g1, rels):
            operands(n, kc, rel)

    def mix_group(ns):
        local = []
        for n in ns:
            rows = chunk_rows(n)
            v = i_ref[0, rows, :]
            per_head = []
            for sl in heads():
                a = jnp.where(causal, _dot_nt(qd_ref[rows, sl], kd_ref[rows, sl]), 0.0)
                per_head.append((a.astype(BF16), _dot_tn(v[:, sl], kst_ref[rows, sl])))
            local.append((rows, v, per_head))
        for n, (rows, v, per_head) in zip(ns, local):
            d_last = dl_ref[decay_rows(n), :]
            for h, (sl, (a, upd)) in enumerate(zip(heads(), per_head)):
                st = st_ref[h]
                of_ref[rows, sl] = (jnp.dot(a, v[:, sl], preferred_element_type=F32)
                                    + _dot_nt(qin_ref[rows, sl], st.astype(BF16)))
                st_new = st.reshape(HGRN_DV // SUBLANES, SUBLANES, HGRN_DK) * d_last[None, :, sl]
                st_ref[h] = st_new.reshape(HGRN_DV, HGRN_DK) + upd

    def norm_block(n):
        rows = chunk_rows(n)
        hg = g_ref[0, rows, :].astype(F32)
        gate = hg + hg * jnp.tanh(hg)
        for sl in heads():
            o = of_ref[rows, sl]
            ms = jnp.mean(o * o, axis=-1, keepdims=True)
            o_ref[0, rows, sl] = (o * lax.rsqrt(ms + RMS_EPS) * nw * gate[:, sl]).astype(o_ref.dtype)

    groups = [list(range(g * HGRN_GROUP, (g + 1) * HGRN_GROUP)) for g in range(n_chunks // HGRN_GROUP)]
    for step in range(len(groups) + 2):
        if step < len(groups):
            prep_group(groups[step])
        if 0 <= step - 1 < len(groups):
            mix_group(groups[step - 1])
        if 0 <= step - 2 < len(groups):
            for n in groups[step - 2]:
                norm_block(n)


def _hgrn(proj3, lb, norm_w):
    B, S, _ = proj3.shape
    ts = HGRN_TILE
    W = HGRN_WIDTH
    spec = lambda j: pl.BlockSpec((1, ts, W), lambda b, s, j=j: (b, s, j))
    vmem = 5 * 2 * ts * W * 2 + HGRN_HEADS * HGRN_DK * HGRN_DV * 4 + (16 << 20)
    return pl.pallas_call(
        _hgrn_kernel,
        grid=(B, S // ts),
        in_specs=[spec(0), spec(1), spec(2), spec(3),
                  pl.BlockSpec((1, W), lambda b, s: (0, 0)),
                  pl.BlockSpec((1, HGRN_DV), lambda b, s: (0, 0))],
        out_specs=pl.BlockSpec((1, ts, W), lambda b, s: (b, s, 0)),
        out_shape=jax.ShapeDtypeStruct((B, S, W), BF16),
        scratch_shapes=[pltpu.VMEM((HGRN_HEADS, HGRN_DV, HGRN_DK), F32),
                        pltpu.VMEM((ts, W), BF16), pltpu.VMEM((ts, W), BF16),
                        pltpu.VMEM((ts, W), BF16), pltpu.VMEM((ts, W), BF16),
                        pltpu.VMEM((ts // CHUNK * SUBLANES, W), F32),
                        pltpu.VMEM((ts, W), F32)],
        compiler_params=pltpu.CompilerParams(
            dimension_semantics=("arbitrary", "arbitrary"), vmem_limit_bytes=_vmem_limit(vmem)),
        name="hgrn",
    )(proj3, proj3, proj3, proj3, lb, norm_w)


def _attn_schedule(nq):
    pairs = [(qi, kj) for qi in range(1, nq) for kj in range(qi)]
    nxt = pairs[1:] + [(0, 0)]
    return np.array([[p[0] for p in nxt], [p[1] for p in nxt],
                     [p[0] for p in pairs], [p[1] for p in pairs]], np.int32)


def _attn_kernel(sched_ref, q_ref, k_ref, v_ref, diag_ref, near_ref, lam_ref, nw_ref, o_ref,
                 s0_ref, s1_ref, m_ref, acc_ref, *, block, out_scale):
    tq = block
    half = tq // 2
    nq = q_ref.shape[1] // tq
    n_off = sched_ref.shape[1]
    ones = jnp.ones((block, LANES), BF16)
    lam = lam_ref[...]

    def blk(j):
        return pl.ds(pl.multiple_of(j * block, block), block)

    def stacked_queries(qi):
        q12 = q_ref[0, blk(qi), :]
        lane = lax.broadcasted_iota(jnp.int32, q12.shape, 1)
        zero = jnp.zeros_like(q12)
        q1 = jnp.where(lane < DIFF_DH, q12, zero)
        q2 = jnp.where(lane >= DIFF_DH, q12, zero)
        return jnp.concatenate([q1[:half], q2[:half], q1[half:], q2[half:]], axis=0)

    def values(kj):
        return jnp.concatenate([v_ref[0, blk(kj), :], ones], axis=1)

    def online_softmax(s, rows, qi, v_ext):
        n_lane_tiles = s.shape[1] // LANES
        mx = s[:, :LANES]
        for c in range(1, n_lane_tiles):
            mx = jnp.maximum(mx, s[:, c * LANES:(c + 1) * LANES])
        m_prev = m_ref[qi, rows, :]
        m_new = jnp.maximum(m_prev, jnp.max(mx, axis=-1, keepdims=True))
        alpha = jnp.exp2(m_prev - m_new)
        p = jnp.exp2(s - jnp.concatenate([m_new] * n_lane_tiles, axis=1)).astype(BF16)
        pv = jnp.dot(p, v_ext, preferred_element_type=F32)
        acc_ref[qi, rows, :] = jnp.concatenate([alpha, alpha], axis=1) * acc_ref[qi, rows, :] + pv
        m_ref[qi, rows, :] = m_new

    @pl.when((pl.program_id(0) == 0) & (pl.program_id(1) == 0))
    def _():
        m_ref[...] = jnp.full_like(m_ref, MASK_VALUE)
        acc_ref[...] = jnp.zeros_like(acc_ref)

    def off_scores(qi, kj, s_ref):
        s_ref[...] = _dot_nt(stacked_queries(qi), k_ref[0, blk(kj), :])

    def off_update(step, s_ref):
        qi = sched_ref[2, step]
        kj = sched_ref[3, step]
        corner = jnp.where(qi - kj == 1, 1.0, 0.0) * near_ref[0, 0]
        for r0 in (0, half):
            s_ref[r0:r0 + LANES, tq - LANES:] += corner
        online_softmax(s_ref[...], slice(0, 2 * tq), qi, values(kj))

    unroll = max(u for u in range(2, ATT_UNROLL + 1, 2) if n_off % u == 0)

    def off_steps(i, carry):
        for j in range(0, unroll, 2):
            step = unroll * i + j
            off_scores(sched_ref[0, step], sched_ref[1, step], s1_ref)
            off_update(step, s0_ref)
            off_scores(sched_ref[0, step + 1], sched_ref[1, step + 1], s0_ref)
            off_update(step + 1, s1_ref)
        return carry

    off_scores(1, 0, s0_ref)
    lax.fori_loop(0, n_off // unroll, off_steps, 0)

    def diag_scores(qi, s_ref):
        qz = stacked_queries(qi)
        s_ref[:tq, :half] = _dot_nt(qz[:tq], k_ref[0, pl.ds(pl.multiple_of(qi * tq, tq), half), :])
        s_ref[tq:, :] = _dot_nt(qz[tq:], k_ref[0, blk(qi), :])

    def diag_update(qi, s_ref):
        bias = diag_ref[0, 0]
        v_ext = values(qi)
        bias_a = bias[:half, :half]
        bias_b = bias[half:, :]
        for r0 in (0, half):
            s_ref[r0:r0 + half, :half] += bias_a
            s_ref[tq + r0:tq + r0 + half, :] += bias_b
        online_softmax(s_ref[:tq, :half], slice(0, tq), qi, v_ext[:half])
        online_softmax(s_ref[tq:, :], slice(tq, 2 * tq), qi, v_ext)
        acc = acc_ref[qi]
        o_all = acc[:, :DIFF_DV] / acc[:, DIFF_DV:]
        o = jnp.concatenate([o_all[:half] - lam * o_all[half:tq],
                             o_all[tq:tq + half] - lam * o_all[tq + half:]], axis=0)
        ms = jnp.mean(o * o, axis=-1, keepdims=True)
        y = o * lax.rsqrt(ms + RMS_EPS) * nw_ref[...] * out_scale
        o_ref[0, blk(qi), :] = y.astype(o_ref.dtype)
        m_ref[qi] = jnp.full((2 * tq, LANES), MASK_VALUE, F32)
        acc_ref[qi] = jnp.zeros_like(acc)

    def diag_pair(i, carry):
        diag_scores(jnp.minimum(2 * i + 1, nq - 1), s1_ref)
        diag_update(2 * i, s0_ref)
        diag_scores(jnp.minimum(2 * i + 2, nq - 1), s0_ref)
        diag_update(2 * i + 1, s1_ref)
        return carry

    lax.fori_loop(0, nq // 2, diag_pair, 0)


def _attn(proj3, bias_tiles, lam, norm_w, *, out_scale):
    B, S, _ = proj3.shape
    H = DIFF_HEADS
    block = ATT_BLOCK
    nq = S // block
    sched = _attn_schedule(nq)
    assert nq % 2 == 0 and sched.shape[1] % 2 == 0, "pipeline steps are unrolled in pairs"
    assert bias_tiles.shape[1] == 2 and _bias_reach() <= LANES <= block // 2
    q_blk = 4 * HGRN_HEADS
    k_blk = q_blk + H
    v_blk = k_blk + H
    seq_spec = lambda j: pl.BlockSpec((1, S, LANES), lambda b, h, j=j: (b, 0, j + h))
    rows = 2 * block
    scratch = [pltpu.VMEM((rows, block), F32), pltpu.VMEM((rows, block), F32),
               pltpu.VMEM((nq, rows, LANES), F32),
               pltpu.VMEM((nq, rows, 2 * LANES), F32)]
    scratch_bytes = rows * (2 * block * 4 + nq * 3 * LANES * 4)
    vmem = (4 * 2 * S * LANES * 2 + 2 * block * block * 4 + scratch_bytes
            + 5 * rows * block * 4 + (6 << 20))
    return pl.pallas_call(
        functools.partial(_attn_kernel, block=block, out_scale=out_scale),
        grid=(B, H),
        in_specs=[
            pl.BlockSpec(memory_space=pltpu.SMEM),
            seq_spec(q_blk), seq_spec(k_blk), seq_spec(v_blk),
            pl.BlockSpec((1, 1, block, block), lambda b, h: (h, 0, 0, 0)),
            pl.BlockSpec((1, 1, LANES, LANES), lambda b, h: (h, 1, 0, block // LANES - 1)),
            pl.BlockSpec((1, LANES), lambda b, h: (0, 0)),
            pl.BlockSpec((1, DIFF_DV), lambda b, h: (0, 0)),
        ],
        out_specs=pl.BlockSpec((1, S, LANES), lambda b, h: (b, 0, h)),
        out_shape=jax.ShapeDtypeStruct((B, S, DIFF_WIDTH), BF16),
        scratch_shapes=scratch,
        compiler_params=pltpu.CompilerParams(
            dimension_semantics=("arbitrary", "arbitrary"),
            vmem_limit_bytes=_vmem_limit(vmem)),
        name="attn",
    )(jnp.asarray(sched), proj3, proj3, proj3, bias_tiles, bias_tiles, lam, norm_w)


def _out_ffn_kernel(x_ref, oh_ref, od_ref, ada_ref, wo_ref, g1_ref, b1_ref,
                    wg_ref, wu_ref, wd_ref, g2_ref, b2_ref, o_ref, x1_ref, u_ref, *, alpha):
    gate_m = ada_ref[0, 2:3, :]
    shift_f = ada_ref[0, 3:4, :]
    scale_f = ada_ref[0, 4:5, :]
    gate_f = ada_ref[0, 5:6, :]
    hw = oh_ref.shape[1]
    sub = x_ref.shape[0] // FFN_SUBTILES
    n_chunks = wg_ref.shape[1] // FF_CHUNK

    def rows_of(t):
        return slice(t * sub, (t + 1) * sub)

    def prep(rows):
        mix = (jnp.dot(oh_ref[rows, :], wo_ref[:hw, :], preferred_element_type=F32)
               + jnp.dot(od_ref[rows, :], wo_ref[hw:, :], preferred_element_type=F32))
        x1 = (_layer_norm_rows(alpha * x_ref[rows, :] + (1.0 + gate_m) * mix) * g1_ref[...]
              + b1_ref[...])
        x1_ref[rows, :] = x1
        u_ref[rows, :] = (_layer_norm_rows(x1) * (1.0 + scale_f) + shift_f).astype(BF16)

    def swiglu(rows):
        u = u_ref[rows, :]
        y = jnp.zeros((sub, o_ref.shape[1]), F32)
        for c in range(n_chunks):
            cols = slice(c * FF_CHUNK, (c + 1) * FF_CHUNK)
            a = jnp.dot(u, wg_ref[:, cols], preferred_element_type=F32)
            b = jnp.dot(u, wu_ref[:, cols], preferred_element_type=F32)
            hid = (a * (1.0 / (1.0 + jnp.exp(-a))) * b).astype(BF16)
            y = y + jnp.dot(hid, wd_ref[cols, :], preferred_element_type=F32)
        return y

    def finish(rows, y):
        o_ref[rows, :] = (_layer_norm_rows(alpha * x1_ref[rows, :] + (1.0 + gate_f) * y)
                          * g2_ref[...] + b2_ref[...])

    prep(rows_of(0))
    for t in range(FFN_SUBTILES):
        if t + 1 < FFN_SUBTILES:
            prep(rows_of(t + 1))
        y = swiglu(rows_of(t))
        finish(rows_of(t), y)


def _out_ffn(x2, oh2, od2, ada3, w_out, ln1_g, ln1_b, w_gate, w_up, w_down, ln2_g, ln2_b,
             *, seq, alpha):
    N, D = x2.shape
    tm = ROW_TILE * FFN_SUBTILES
    sub = ROW_TILE
    per_batch = seq // tm
    row = lambda w: pl.BlockSpec((tm, w), lambda i: (i, 0))
    const = lambda shape: pl.BlockSpec(shape, lambda i: (0, 0), pipeline_mode=pl.Buffered(1))
    vec = pl.BlockSpec((1, D), lambda i: (0, 0))
    weights = (w_out.size + w_gate.size + w_up.size + w_down.size) * 2
    vmem = (weights + 2 * 2 * tm * D * 4 + 2 * 2 * tm * oh2.shape[1] * 2 + tm * D * (4 + 2)
            + 4 * sub * D * 4 + 4 * sub * FF_CHUNK * 4 + (4 << 20))
    return pl.pallas_call(
        functools.partial(_out_ffn_kernel, alpha=alpha),
        grid=(N // tm,),
        in_specs=[
            row(D), row(oh2.shape[1]), row(od2.shape[1]),
            pl.BlockSpec((1, ada3.shape[1], D), lambda i: (i // per_batch, 0, 0)),
            const(w_out.shape), vec, vec,
            const(w_gate.shape), const(w_up.shape), const(w_down.shape), vec, vec,
        ],
        out_specs=row(D),
        out_shape=jax.ShapeDtypeStruct((N, D), F32),
        scratch_shapes=[pltpu.VMEM((tm, D), F32), pltpu.VMEM((tm, D), BF16)],
        compiler_params=pltpu.CompilerParams(
            dimension_semantics=("arbitrary",), vmem_limit_bytes=_vmem_limit(vmem)),
        name="out_ffn",
    )(x2, oh2, od2, ada3, w_out, ln1_g, ln1_b, w_gate, w_up, w_down, ln2_g, ln2_b)


def kernel(x, c, w_ada, b_ada, w_in, lb_logits, hgrn_norm_w, lam_q1, lam_k1, lam_q2, lam_k2,
           diff_norm_w, rel_bias, w_out, ln1_g, ln1_b, w_gate, w_up, w_down, ln2_g, ln2_b):
    B, S, D = x.shape
    depth = w_ada.shape[0]
    assert S % (ROW_TILE * max(FFN_SUBTILES, PROJ_SUBTILES)) == 0
    assert S % HGRN_TILE == 0 and S % ATT_BLOCK == 0
    assert HGRN_TILE % (CHUNK * HGRN_GROUP) == 0 and w_in.shape[2] % PROJ_CHUNK == 0
    assert w_gate.shape[2] % FF_CHUNK == 0
    alpha = (2.0 * depth) ** 0.25
    bias_tiles = _bias_tiles(rel_bias, block=ATT_BLOCK, n_tiles=_num_near_tiles(ATT_BLOCK))
    col_scales = ((0, HGRN_WIDTH, HGRN_DK ** -0.5),
                  (HGRN_WIDTH, 2 * HGRN_WIDTH, 0.5),
                  (3 * HGRN_WIDTH, 4 * HGRN_WIDTH, 0.5),
                  (4 * HGRN_WIDTH, 4 * HGRN_WIDTH + DIFF_WIDTH, LOG2E * DIFF_DH ** -0.5))
    for l in range(depth):
        lam_init = 0.8 - 0.6 * math.exp(-0.3 * l)
        ada, lb, lam = _prep(c, w_ada[l], b_ada[l][None, :], lb_logits,
                             lam_q1[l][None, :], lam_k1[l][None, :],
                             lam_q2[l][None, :], lam_k2[l][None, :],
                             layer=l, lam_init=lam_init)
        ada3 = ada.reshape(B, 6, D)
        x2 = x.reshape(B * S, D)
        proj = _in_proj(x2, ada3, w_in[l].astype(BF16), seq=S, col_scales=col_scales)
        proj3 = proj.reshape(B, S, proj.shape[1])
        o_h = _hgrn(proj3, lb, hgrn_norm_w[l][None, :])
        o_d = _attn(proj3, bias_tiles, lam, diff_norm_w[l][None, :],
                    out_scale=1.0 - lam_init)
        out = _out_ffn(x2, o_h.reshape(B * S, -1), o_d.reshape(B * S, -1), ada3,
                       w_out[l].astype(BF16), ln1_g[l][None, :], ln1_b[l][None, :],
                       w_gate[l].astype(BF16), w_up[l].astype(BF16), w_down[l].astype(BF16),
                       ln2_g[l][None, :], ln2_b[l][None, :], seq=S, alpha=alpha)
        x = out.reshape(B, S, D)
    return x
```
